```python
import math
import jax, jax.numpy as jnp
from jax import lax
import numpy as np

D_MODEL = 1024
BATCH = 2
SEQ = 8192
DEPTH = 4

GRID_W = 64
CTX_LEN = 256
N_MIXERS = 3
GDN_HEADS = 8
GDN_DK = 128
GDN_DV = 128
GDN_WIDTH = GDN_HEADS * GDN_DK
GDN_CHUNK = 64
SHORT_CONV = 3
SCONV_WIDTH = D_MODEL
DIFF_HEADS = 8
DIFF_DH = D_MODEL // (2 * DIFF_HEADS)
DIFF_DV = 2 * DIFF_DH
DIFF_WIDTH = DIFF_HEADS * DIFF_DV
Q_BLOCK = 128
ROPE_BASE = 10000.0
D_FF = 4 * D_MODEL
N_ADA = 6
EPS = 1e-6
N_A = (DEPTH + 2) // 3
N_B = (DEPTH + 1) // 3
N_C = DEPTH // 3

kernel_name = 'hybrid_gdn_shortconv_diffattn_dit'

F32 = jnp.float32


def rmsnorm(x, g):
    xf = x.astype(F32)
    y = xf * lax.rsqrt(jnp.mean(xf * xf, axis=-1, keepdims=True) + EPS)
    return (y * g.astype(F32)).astype(x.dtype)


def l2norm(x):
    xf = x.astype(F32)
    return (xf * lax.rsqrt(jnp.sum(xf * xf, axis=-1, keepdims=True) + EPS)).astype(x.dtype)


def modulate(h, shift, scale):
    return h * (1.0 + scale) + shift


def conv_centred(x, w):
    k = w.shape[0]
    pad = k // 2
    t = x.shape[1]
    xp = jnp.pad(x, ((0, 0), (pad, pad), (0, 0)))
    y = xp[:, 0:t] * w[0]
    for j in range(1, k):
        y = y + xp[:, j:j + t] * w[j]
    return y


def axial_rope(n_tokens, dim, dtype):
    n_rows = n_tokens // GRID_W
    row = jnp.repeat(jnp.arange(n_rows), GRID_W).astype(F32)
    col = jnp.tile(jnp.arange(GRID_W), n_rows).astype(F32)
    nf = dim // 4
    inv = ROPE_BASE ** (-jnp.arange(nf, dtype=F32) / nf)
    ang = jnp.concatenate([row[:, None] * inv, col[:, None] * inv], axis=-1)
    return jnp.cos(ang).astype(dtype), jnp.sin(ang).astype(dtype)


def apply_rope(x, cos, sin):
    xp = x.reshape(x.shape[:-1] + (x.shape[-1] // 2, 2))
    x0, x1 = xp[..., 0], xp[..., 1]
    shape = (1, cos.shape[0]) + (1,) * (x.ndim - 3) + (cos.shape[1],)
    c, s = cos.reshape(shape), sin.reshape(shape)
    return jnp.stack([x0 * c - x1 * s, x0 * s + x1 * c], axis=-1).reshape(x.shape)


def mlp_sqrelu(h, w1, w2):
    return jnp.square(jax.nn.relu(h @ w1)) @ w2


def gated_delta_chunked(q, k, v, g, beta, s0):
    out_dtype = v.dtype
    b, h, t, dk = q.shape
    dv = v.shape[-1]
    c = GDN_CHUNK
    n = t // c
    q = q.astype(F32).reshape(b, h, n, c, dk)
    k = k.astype(F32).reshape(b, h, n, c, dk)
    v = v.astype(F32).reshape(b, h, n, c, dv)
    g = g.astype(F32).reshape(b, h, n, c)
    beta = beta.astype(F32).reshape(b, h, n, c)
    gc = jnp.cumsum(g, axis=-1)
    incl = jnp.tril(jnp.ones((c, c), dtype=bool))
    strict = jnp.tril(jnp.ones((c, c), dtype=bool), -1)
    decay = jnp.where(incl, jnp.exp(jnp.where(incl, gc[..., :, None] - gc[..., None, :], 0.0)), 0.0)
    kk = jnp.einsum('bhncd,bhnsd->bhncs', k, k)
    a_mat = jnp.where(strict, beta[..., :, None] * kk * decay, 0.0) + jnp.eye(c, dtype=F32)
    rhs = jnp.concatenate([v * beta[..., None], k * (beta * jnp.exp(gc))[..., None]], axis=-1)
    sol = lax.linalg.triangular_solve(a_mat, rhs, left_side=True, lower=True, unit_diagonal=True)
    u, w = sol[..., :dv], sol[..., dv:]
    attn = jnp.where(incl, jnp.einsum('bhncd,bhnsd->bhncs', q, k) * decay, 0.0)
    qd = q * jnp.exp(gc)[..., None]
    kt = k * jnp.exp(gc[..., -1:] - gc)[..., None]
    cd = jnp.exp(gc[..., -1])
    xs = tuple(jnp.moveaxis(z, 2, 0) for z in (qd, kt, u, w, attn, cd))

    def step(s, inp):
        qd_c, kt_c, u_c, w_c, attn_c, cd_c = inp
        v_new = u_c - jnp.einsum('bhcd,bhde->bhce', w_c, s)
        o_c = jnp.einsum('bhcd,bhde->bhce', qd_c, s) + jnp.einsum('bhcs,bhse->bhce', attn_c, v_new)
        s = s * cd_c[..., None, None] + jnp.einsum('bhcd,bhce->bhde', kt_c, v_new)
        return s, o_c

    s_fin, o = lax.scan(step, s0.astype(F32), xs)
    o = jnp.moveaxis(o, 0, 2).reshape(b, h, t, dv)
    return o.astype(out_dtype), s_fin


def gdn_mixer(h_ctx, h_lat, w_in, conv_w, a_log, dt_bias, norm_g, w_out):
    def prep(hh):
        b, t, _ = hh.shape
        z = hh @ w_in
        qkv = jax.nn.silu(conv_centred(z[..., :3 * GDN_WIDTH], conv_w))
        gate = z[..., 3 * GDN_WIDTH:4 * GDN_WIDTH]
        ab = z[..., 4 * GDN_WIDTH:].reshape(b, t, 2, 2, GDN_HEADS)
        heads = lambda y: y.reshape(b, t, GDN_HEADS, -1).transpose(0, 2, 1, 3)
        q = l2norm(heads(qkv[..., :GDN_WIDTH])) * (GDN_DK ** -0.5)
        k = l2norm(heads(qkv[..., GDN_WIDTH:2 * GDN_WIDTH]))
        v = heads(qkv[..., 2 * GDN_WIDTH:])
        g = -jnp.exp(a_log.astype(F32)) * jax.nn.softplus(ab[:, :, 0].astype(F32) + dt_bias.astype(F32))
        beta = jax.nn.sigmoid(ab[:, :, 1].astype(F32))
        return q, k, v, g.transpose(2, 0, 3, 1), beta.transpose(2, 0, 3, 1), gate

    qc, kc, vc, gcx, bcx, zc = prep(h_ctx)
    ql, kl, vl, glt, blt, zl = prep(h_lat)
    s0 = jnp.zeros((h_lat.shape[0], GDN_HEADS, GDN_DK, GDN_DV), F32)
    flip = lambda y: jnp.flip(y, axis=2)
    oc_f, sc_f = gated_delta_chunked(qc, kc, vc, gcx[0], bcx[0], s0)
    ol_f, _ = gated_delta_chunked(ql, kl, vl, glt[0], blt[0], sc_f)
    oc_b, sc_b = gated_delta_chunked(flip(qc), flip(kc), flip(vc), flip(gcx[1]), flip(bcx[1]), s0)
    ol_b, _ = gated_delta_chunked(flip(ql), flip(kl), flip(vl), flip(glt[1]), flip(blt[1]), sc_b)
    o_ctx = oc_f + flip(oc_b)
    o_lat = ol_f + flip(ol_b)

    def out(o, z):
        b, h, t, dv = o.shape
        o = o.transpose(0, 2, 1, 3)
        o = rmsnorm(o, norm_g) * jax.nn.silu(z.reshape(b, t, h, dv))
        return o.reshape(b, t, h * dv) @ w_out

    return out(o_ctx, zc), out(o_lat, zl)


def short_conv_mixer(h_ctx, h_lat, w_in, conv_w, w_out):
    def mix(hh):
        z = hh @ w_in
        bg, cg, hv = jnp.split(z, 3, axis=-1)
        return (bg * conv_centred(cg * hv, conv_w)) @ w_out
    return mix(h_ctx), mix(h_lat)


def diff_attn_mixer(h_ctx, h_lat, w_in, lam, norm_g, w_out, lambda_init):
    def prep(hh):
        b, t, _ = hh.shape
        z = hh @ w_in
        q = z[..., :DIFF_WIDTH].reshape(b, t, DIFF_HEADS, 2, DIFF_DH)
        k = z[..., DIFF_WIDTH:2 * DIFF_WIDTH].reshape(b, t, DIFF_HEADS, 2, DIFF_DH)
        v = z[..., 2 * DIFF_WIDTH:].reshape(b, t, DIFF_HEADS, DIFF_DV)
        return q, k, v

    qc, kc, vc = prep(h_ctx)
    ql, kl, vl = prep(h_lat)
    b, t = h_lat.shape[0], h_lat.shape[1]
    cos, sin = axial_rope(t, DIFF_DH, ql.dtype)
    ql = apply_rope(ql, cos, sin)
    kl = apply_rope(kl, cos, sin)
    lf = lam.astype(F32)
    lmb = jnp.exp(jnp.sum(lf[0] * lf[1])) - jnp.exp(jnp.sum(lf[2] * lf[3])) + lambda_init
    scale = DIFF_DH ** -0.5

    def attend(q, k, v):
        s = jnp.einsum('bqhmd,bkhmd->bhmqk', q, k).astype(F32) * scale
        p = jax.nn.softmax(s, axis=-1)
        a = (p[:, :, 0] - lmb * p[:, :, 1]).astype(v.dtype)
        return jnp.einsum('bhqk,bkhe->bqhe', a, v)

    o_ctx = attend(qc, kc, vc)
    k_all = jnp.concatenate([kc, kl], axis=1)
    v_all = jnp.concatenate([vc, vl], axis=1)
    nb = t // Q_BLOCK
    qb = jnp.moveaxis(ql.reshape(b, nb, Q_BLOCK, DIFF_HEADS, 2, DIFF_DH), 1, 0)
    o_lat = lax.map(lambda qq: attend(qq, k_all, v_all), qb)
    o_lat = jnp.moveaxis(o_lat, 0, 1).reshape(b, t, DIFF_HEADS, DIFF_DV)

    def out(o):
        o = rmsnorm(o, norm_g) * (1.0 - lambda_init)
        return o.reshape(o.shape[0], o.shape[1], DIFF_WIDTH) @ w_out

    return out(o_ctx), out(o_lat)


def setup_inputs(seed: int = 0) -> dict:
    key = jax.random.key(seed)
    ks = jax.random.split(key, 24)
    D = D_MODEL

    def nrm(k, shape, s):
        return jax.random.normal(k, shape, F32) * s

    gdn_cols = 4 * GDN_WIDTH + 4 * GDN_HEADS
    dt = jnp.exp(jax.random.uniform(ks[13], (N_A, 2, GDN_HEADS), F32, math.log(1e-3), math.log(1e-1)))
    return {
        'x': nrm(ks[0], (BATCH, SEQ, D), 1.0),
        'c': nrm(ks[1], (BATCH, D), 1.0),
        'ctx': nrm(ks[2], (BATCH, CTX_LEN, D), 1.0),
        'c_ctx': nrm(ks[3], (D,), 1.0),
        'norm1_g': 1.0 + nrm(ks[4], (DEPTH, D), 0.02),
        'norm2_g': 1.0 + nrm(ks[5], (DEPTH, D), 0.02),
        'ada_w': nrm(ks[6], (DEPTH, D, N_ADA * D), 0.5 * D ** -0.5),
        'ada_b': nrm(ks[7], (DEPTH, N_ADA * D), 0.02),
        'mlp_w1': nrm(ks[8], (DEPTH, D, D_FF), D ** -0.5),
        'mlp_w2': nrm(ks[9], (DEPTH, D_FF, D), D_FF ** -0.5),
        'gdn_w_in': nrm(ks[10], (N_A, D, gdn_cols), D ** -0.5),
        'gdn_conv': nrm(ks[11], (N_A, SHORT_CONV, 3 * GDN_WIDTH), SHORT_CONV ** -0.5),
        'gdn_a_log': jnp.log(jax.random.uniform(ks[12], (N_A, 2, GDN_HEADS), F32, 1.0, 16.0)),
        'gdn_dt_bias': dt + jnp.log(-jnp.expm1(-dt)),
        'gdn_norm_g': 1.0 + nrm(ks[14], (N_A, GDN_DV), 0.02),
        'gdn_w_out': nrm(ks[15], (N_A, GDN_WIDTH, D), GDN_WIDTH ** -0.5),
        'sconv_w_in': nrm(ks[16], (N_B, D, 3 * SCONV_WIDTH), D ** -0.5),
        'sconv_conv': nrm(ks[17], (N_B, SHORT_CONV, SCONV_WIDTH), SHORT_CONV ** -0.5),
        'sconv_w_out': nrm(ks[18], (N_B, SCONV_WIDTH, D), SCONV_WIDTH ** -0.5),
        'diff_w_in': nrm(ks[19], (N_C, D, 3 * DIFF_WIDTH), D ** -0.5),
        'diff_lambda': nrm(ks[20], (N_C, 4, DIFF_DH), 0.1),
        'diff_norm_g': 1.0 + nrm(ks[21], (N_C, DIFF_DV), 0.02),
        'diff_w_out': nrm(ks[22], (N_C, DIFF_WIDTH, D), DIFF_WIDTH ** -0.5),
        'final_g': 1.0 + nrm(ks[23], (D,), 0.02),
    }


def reference(x, c, ctx, c_ctx, norm1_g, norm2_g, ada_w, ada_b, mlp_w1, mlp_w2,
              gdn_w_in, gdn_conv, gdn_a_log, gdn_dt_bias, gdn_norm_g, gdn_w_out,
              sconv_w_in, sconv_conv, sconv_w_out,
              diff_w_in, diff_lambda, diff_norm_g, diff_w_out, final_g):
    s_lat = jax.nn.silu(c)
    s_ctx = jax.nn.silu(c_ctx)
    x_ctx = ctx
    for i in range(DEPTH):
        kind, j = i % N_MIXERS, i // N_MIXERS
        last = i == DEPTH - 1
        mod_lat = (s_lat @ ada_w[i] + ada_b[i])[:, None, :]
        mod_ctx = (s_ctx @ ada_w[i] + ada_b[i])[None, None, :]
        sh1, sc1, g1, sh2, sc2, g2 = jnp.split(mod_lat, N_ADA, axis=-1)
        csh1, csc1, cg1, csh2, csc2, cg2 = jnp.split(mod_ctx, N_ADA, axis=-1)
        h_lat = modulate(rmsnorm(x, norm1_g[i]), sh1, sc1)
        h_ctx = modulate(rmsnorm(x_ctx, norm1_g[i]), csh1, csc1)
        if kind == 0:
            y_ctx, y_lat = gdn_mixer(h_ctx, h_lat, gdn_w_in[j], gdn_conv[j], gdn_a_log[j],
                                     gdn_dt_bias[j], gdn_norm_g[j], gdn_w_out[j])
        elif kind == 1:
            y_ctx, y_lat = short_conv_mixer(h_ctx, h_lat, sconv_w_in[j], sconv_conv[j], sconv_w_out[j])
        else:
            lambda_init = 0.8 - 0.6 * math.exp(-0.3 * i)
            y_ctx, y_lat = diff_attn_mixer(h_ctx, h_lat, diff_w_in[j], diff_lambda[j],
                                           diff_norm_g[j], diff_w_out[j], lambda_init)
        x = x + g1 * y_lat
        x = x + g2 * mlp_sqrelu(modulate(rmsnorm(x, norm2_g[i]), sh2, sc2), mlp_w1[i], mlp_w2[i])
        if not last:
            x_ctx = x_ctx + cg1 * y_ctx
            x_ctx = x_ctx + cg2 * mlp_sqrelu(modulate(rmsnorm(x_ctx, norm2_g[i]), csh2, csc2),
                                             mlp_w1[i], mlp_w2[i])
    return rmsnorm(x, final_g)
```

```python
import functools
import math

import jax
import jax.numpy as jnp
from jax import lax
from jax.experimental import pallas as pl
from jax.experimental.pallas import tpu as pltpu

F32 = jnp.float32
BF16 = jnp.bfloat16

D = 1024
BATCH = 2
SEQ = 8192
CTX = 256
DEPTH = 4
N_ADA = 6
D_FF = 4 * D
EPS = 1e-6
HEADS = 8
HEAD_DIM = 128
CHUNK = 64
DIFF_DH = 64
GRID_W = 64
ROPE_BASE = 10000.0

LAT_ROWS = BATCH * SEQ
CTX_ROWS = BATCH * CTX
ROWS = LAT_ROWS + CTX_ROWS
TM = 512
N_TILES = ROWS // TM
N_LAT_TILES = LAT_ROWS // TM
TILES_PER_BATCH = SEQ // TM
VMEM_LIMIT = 56 * 1024 * 1024


def _params(sem, vmem=VMEM_LIMIT):
    return pltpu.CompilerParams(dimension_semantics=sem, vmem_limit_bytes=vmem)


def _mod_row(i):
    return jnp.where(i >= N_LAT_TILES, 0, 1 + i // TILES_PER_BATCH)


def _mod_spec(k):
    return pl.BlockSpec((None, None, 1, D), lambda i, *_: (_mod_row(i), k, 0, 0))


def _norm_mod(x, g, sh, sc):
    ms = jnp.mean(x * x, axis=-1, keepdims=True)
    return (x * lax.rsqrt(ms + EPS) * g) * (1.0 + sc) + sh


def _head_rms(o, g):
    outs = []
    for h in range(HEADS):
        oh = o[:, h * HEAD_DIM:(h + 1) * HEAD_DIM]
        ms = jnp.mean(oh * oh, axis=-1, keepdims=True)
        outs.append(oh * lax.rsqrt(ms + EPS) * g)
    return jnp.concatenate(outs, axis=1)


def _seq_edges(r0, tm):
    rows = r0 + lax.broadcasted_iota(jnp.int32, (tm, 1), 0)
    starts = (0, SEQ, LAT_ROWS, LAT_ROWS + CTX)
    is_start = functools.reduce(jnp.logical_or, [rows == s for s in starts])
    is_end = functools.reduce(jnp.logical_or, [rows == s + n - 1 for s, n in
                                               zip(starts, (SEQ, SEQ, CTX, CTX))])
    return is_start, is_end


def _conv3(x, prev8, next8, w, r0):
    tm = x.shape[0]
    loc = lax.broadcasted_iota(jnp.int32, (tm, 1), 0)
    is_start, is_end = _seq_edges(r0, tm)
    x_dn = jnp.where(loc == 0, prev8[7:8, :], pltpu.roll(x, 1, 0))
    x_dn = jnp.where(is_start, 0.0, x_dn)
    x_up = jnp.where(loc == tm - 1, next8[0:1, :], pltpu.roll(x, tm - 1, 0))
    x_up = jnp.where(is_end, 0.0, x_up)
    return x_dn * w[0:1, :] + x * w[1:2, :] + x_up * w[2:3, :]


def _halo_specs(tm, tc, col_of):
    nb = ROWS // 8
    prev = pl.BlockSpec((8, tc), lambda i, *a: (jnp.maximum(i * (tm // 8) - 1, 0), col_of(i, *a)))
    nxt = pl.BlockSpec((8, tc), lambda i, *a: (jnp.minimum((i + 1) * (tm // 8), nb - 1), col_of(i, *a)))
    return prev, nxt


def _ada_kernel(s_ref, w_ref, b_ref, o_ref):
    s = s_ref[...]
    s = s * jax.nn.sigmoid(s)
    o_ref[...] = jnp.dot(s.astype(BF16), w_ref[...].astype(BF16),
                         preferred_element_type=F32) + b_ref[...]


def _ada(c, c_ctx, ada_w, ada_b):
    tn = 1536
    s = jnp.concatenate([c_ctx[None, :], c, jnp.zeros((5, D), F32)], axis=0)
    out = pl.pallas_call(
        _ada_kernel,
        grid=(DEPTH, N_ADA * D // tn),
        in_specs=[pl.BlockSpec((8, D), lambda l, j: (0, 0)),
                  pl.BlockSpec((None, D, tn), lambda l, j: (l, 0, j)),
                  pl.BlockSpec((None, 1, tn), lambda l, j: (l, 0, j))],
        out_specs=pl.BlockSpec((None, 8, tn), lambda l, j: (l, 0, j)),
        out_shape=jax.ShapeDtypeStruct((DEPTH, 8, N_ADA * D), F32),
        compiler_params=_params(("parallel", "parallel")),
        name="ada",
    )(s, ada_w, ada_b.reshape(DEPTH, 1, N_ADA * D))
    return out[:, :3].reshape(DEPTH, 3, N_ADA, 1, D)


def _proj_in_kernel(x_ref, g_ref, sh_ref, sc_ref, w_ref, o_ref, h_ref):
    @pl.when(pl.program_id(1) == 0)
    def _():
        h_ref[...] = _norm_mod(x_ref[...], g_ref[...], sh_ref[...], sc_ref[...]).astype(BF16)

    o_ref[...] = jnp.dot(h_ref[...], w_ref[...], preferred_element_type=F32)


def _proj_in(x, g, mods, w, tn):
    n = w.shape[1]
    return pl.pallas_call(
        _proj_in_kernel,
        grid=(N_TILES, n // tn),
        in_specs=[pl.BlockSpec((TM, D), lambda i, j: (i, 0)),
                  pl.BlockSpec((1, D), lambda i, j: (0, 0)),
                  _mod_spec(0), _mod_spec(1),
                  pl.BlockSpec((D, tn), lambda i, j: (0, j))],
        out_specs=pl.BlockSpec((TM, tn), lambda i, j: (i, j)),
        out_shape=jax.ShapeDtypeStruct((ROWS, n), F32),
        scratch_shapes=[pltpu.VMEM((TM, D), BF16)],
        compiler_params=_params(("parallel", "arbitrary")),
        name="proj_in",
    )(x, g.reshape(1, D), mods, mods, w)


def _mlp_kernel(x_ref, g_ref, sh_ref, sc_ref, gate_ref, w1_ref, w2_ref, fg_ref, o_ref,
                h_ref, acc_ref, *, final_norm):
    j = pl.program_id(1)

    @pl.when(j == 0)
    def _():
        h_ref[...] = _norm_mod(x_ref[...], g_ref[...], sh_ref[...], sc_ref[...]).astype(BF16)
        acc_ref[...] = jnp.zeros_like(acc_ref)

    a = jnp.dot(h_ref[...], w1_ref[...], preferred_element_type=F32)
    a = jnp.square(jnp.maximum(a, 0.0))
    acc_ref[...] += jnp.dot(a.astype(BF16), w2_ref[...], preferred_element_type=F32)

    @pl.when(j == pl.num_programs(1) - 1)
    def _():
        y = x_ref[...] + gate_ref[...] * acc_ref[...]
        if final_norm:
            ms = jnp.mean(y * y, axis=-1, keepdims=True)
            y = y * lax.rsqrt(ms + EPS) * fg_ref[...]
        o_ref[...] = y


def _mlp(x, g, mods, w1, w2, final_g, *, last):
    tf = 1024
    n_tiles = N_LAT_TILES if last else N_TILES
    return pl.pallas_call(
        functools.partial(_mlp_kernel, final_norm=last),
        grid=(n_tiles, D_FF // tf),
        in_specs=[pl.BlockSpec((TM, D), lambda i, j: (i, 0)),
                  pl.BlockSpec((1, D), lambda i, j: (0, 0)),
                  _mod_spec(3), _mod_spec(4), _mod_spec(5),
                  pl.BlockSpec((D, tf), lambda i, j: (0, j)),
                  pl.BlockSpec((tf, D), lambda i, j: (j, 0)),
                  pl.BlockSpec((1, D), lambda i, j: (0, 0))],
        out_specs=pl.BlockSpec((TM, D), lambda i, j: (i, 0)),
        out_shape=jax.ShapeDtypeStruct((n_tiles * TM, D), F32),
        scratch_shapes=[pltpu.VMEM((TM, D), BF16), pltpu.VMEM((TM, D), F32)],
        compiler_params=_params(("parallel", "arbitrary")),
        name="mlp",
    )(x, g.reshape(1, D), mods, mods, mods, w1, w2, final_g.reshape(1, D))


GDN_QKV = 3 * D
GDN_GATE_COL = 3
GDN_AB_COL = 4


def _gdn_prep_kernel(z_ref, zp_ref, zn_ref, cw_ref, ab_ref, alog_ref, dtb_ref,
                     qkv_ref, gates_ref, gates_t_ref):
    i, j = pl.program_id(0), pl.program_id(1)
    y = _conv3(z_ref[...], zp_ref[...], zn_ref[...], cw_ref[...], i * TM)
    y = y * jax.nn.sigmoid(y)

    @pl.when(j < 2)
    def _():
        scale = jnp.where(j == 0, HEAD_DIM ** -0.5, 1.0).astype(F32)
        outs = []
        for h in range(HEADS):
            yh = y[:, h * HEAD_DIM:(h + 1) * HEAD_DIM]
            ss = jnp.sum(yh * yh, axis=-1, keepdims=True)
            outs.append(yh * (lax.rsqrt(ss + EPS) * scale))
        qkv_ref[...] = jnp.concatenate(outs, axis=1)

    @pl.when(j == 2)
    def _():
        qkv_ref[...] = y

    @pl.when(j == 0)
    def _():
        ab = ab_ref[...]
        lane = lax.broadcasted_iota(jnp.int32, ab.shape, 1) & (HEAD_DIM - 1)
        x = ab + dtb_ref[...]
        softplus = jnp.maximum(x, 0.0) + jnp.log1p(jnp.exp(-jnp.abs(x)))
        gates = jnp.where(lane < 2, -jnp.exp(alog_ref[...]) * softplus, jax.nn.sigmoid(ab))
        gates_ref[...] = gates
        for h in range(HEADS):
            gt = gates[:, h * HEAD_DIM:(h + 1) * HEAD_DIM].T
            gates_t_ref[h * 8:(h + 1) * 8, :] = gt[0:8, :]


def _gdn_prep(z, conv_w, alog_pad, dtb_pad):
    col = lambda i, j: j
    prev, nxt = _halo_specs(TM, D, col)
    return pl.pallas_call(
        _gdn_prep_kernel,
        grid=(N_TILES, 3),
        in_specs=[pl.BlockSpec((TM, D), lambda i, j: (i, j)), prev, nxt,
                  pl.BlockSpec((3, D), lambda i, j: (0, j)),
                  pl.BlockSpec((TM, D), lambda i, j: (i, GDN_AB_COL)),
                  pl.BlockSpec((1, D), lambda i, j: (0, 0)),
                  pl.BlockSpec((1, D), lambda i, j: (0, 0))],
        out_specs=[pl.BlockSpec((TM, D), lambda i, j: (i, j)),
                   pl.BlockSpec((TM, D), lambda i, j: (i, 0)),
                   pl.BlockSpec((8 * HEADS, TM), lambda i, j: (0, i))],
        out_shape=[jax.ShapeDtypeStruct((ROWS, GDN_QKV), F32),
                   jax.ShapeDtypeStruct((ROWS, D), F32),
                   jax.ShapeDtypeStruct((8 * HEADS, ROWS), F32)],
        compiler_params=_params(("parallel", "arbitrary")),
        name="gdn_prep",
    )(z, z, z, conv_w, z, alog_pad, dtb_pad)


GDN_TB = 256
GDN_NC = GDN_TB // CHUNK
GDN_HG = 2
GDN_STEPS = (CTX + SEQ) // GDN_TB
GDN_CTX_BLK = LAT_ROWS // GDN_TB
GDN_LAT_BLKS = SEQ // GDN_TB


def _split_dot(a_exact, x, *, x_is_lhs=False):
    hi = x.astype(BF16)
    r1 = x - hi.astype(F32)
    mid = r1.astype(BF16)
    lo = (r1 - mid.astype(F32)).astype(BF16)
    out = None
    for part in (hi, mid, lo):
        t = (jnp.dot(part, a_exact, preferred_element_type=F32) if x_is_lhs
             else jnp.dot(a_exact, part, preferred_element_type=F32))
        out = t if out is None else out + t
    return out


TRI_BASE_SHIFT = 3


def _tri_masks(r, c):
    same = lambda shift: (r >> shift) == (c >> shift)
    offs = [same(shift + 1) & jnp.logical_not(same(shift)) for shift in range(TRI_BASE_SHIFT, 6)]
    return same(TRI_BASE_SHIFT), offs


def _tri_inverse(lm, eye, masks):
    dot = lambda a, b: jnp.dot(a.astype(BF16), b.astype(BF16), preferred_element_type=F32)
    base, offs = masks
    ld = jnp.where(base, lm, 0.0)
    p = eye - ld
    q = dot(ld, ld)
    pq = dot(jnp.concatenate([q, p], axis=0), q)
    t = p + pq[CHUNK:]
    t = t + dot(t, pq[:CHUNK])
    for off in offs:
        t = t - dot(t, dot(jnp.where(off, lm, 0.0), t))
    return t


def _gdn_scan_kernel(qf_ref, kf_ref, vf_ref, gf_ref, gtf_ref,
                     qb_ref, kb_ref, vb_ref, gb_ref, gtb_ref,
                     of_ref, ob_ref, s_ref):
    @pl.when(pl.program_id(1) == 0)
    def _():
        s_ref[...] = jnp.zeros_like(s_ref)

    r = lax.broadcasted_iota(jnp.int32, (CHUNK, CHUNK), 0)
    c = lax.broadcasted_iota(jnp.int32, (CHUNK, CHUNK), 1)
    eye = (r == c).astype(F32)
    tri_masks = _tri_masks(r, c)
    r2 = lax.broadcasted_iota(jnp.int32, (GDN_TB, GDN_TB), 0)
    c2 = lax.broadcasted_iota(jnp.int32, (GDN_TB, GDN_TB), 1)
    same = (r2 >> 6) == (c2 >> 6)
    lower = jnp.where(same & (r2 >= c2), 1.0, 0.0).astype(BF16)
    upper = jnp.where(same & (r2 <= c2), 1.0, 0.0).astype(BF16)
    nt = (((1,), (1,)), ((), ()))
    tn = (((0,), (0,)), ((), ()))

    dirs = ((qf_ref, kf_ref, vf_ref, gf_ref, gtf_ref, of_ref),
            (qb_ref, kb_ref, vb_ref, gb_ref, gtb_ref, ob_ref))
    for d, (q_ref, k_ref, v_ref, g_ref, gt_ref, o_ref) in enumerate(dirs):
        fwd = d == 0
        incl = (r >= c) if fwd else (r <= c)
        strict = (r > c) if fwd else (r < c)
        g_all = g_ref[...]
        gc_cols = _split_dot(lower if fwd else upper, g_all)
        gc_rows = _split_dot(upper if fwd else lower, gt_ref[...], x_is_lhs=True)
        last = CHUNK - 1 if fwd else 0
        for hl in range(GDN_HG):
            l0 = hl * HEAD_DIM
            s = s_ref[d * GDN_HG + hl]
            for ci in (range(GDN_NC) if fwd else reversed(range(GDN_NC))):
                rows = slice(ci * CHUNK, (ci + 1) * CHUNK)
                q = q_ref[rows, l0:l0 + HEAD_DIM]
                k = k_ref[rows, l0:l0 + HEAD_DIM]
                v = v_ref[rows, l0:l0 + HEAD_DIM]
                gcol = gc_cols[rows, l0 + d:l0 + d + 1]
                beta = g_all[rows, l0 + 2 + d:l0 + 3 + d]
                grow = gc_rows[8 * hl + d:8 * hl + d + 1, rows]
                decay = jnp.where(incl, jnp.exp(jnp.where(incl, gcol - grow, 0.0)), 0.0)
                kq = jnp.concatenate([k, q], axis=0).astype(BF16)
                kkqk = lax.dot_general(kq, k.astype(BF16), nt, preferred_element_type=F32)
                lm = jnp.where(strict, beta * kkqk[:CHUNK] * decay, 0.0)
                attn = jnp.where(incl, kkqk[CHUNK:] * decay, 0.0)
                tinv = _tri_inverse(lm, eye, tri_masks)
                eg = jnp.exp(gcol)
                rhs = jnp.concatenate([v * beta, k * (beta * eg)], axis=1)
                uw = jnp.dot(tinv.astype(BF16), rhs.astype(BF16), preferred_element_type=F32)
                u, w = uw[:, :HEAD_DIM], uw[:, HEAD_DIM:]
                glast = gcol[last:last + 1, :]
                kt = k * jnp.exp(glast - gcol)
                wq = jnp.concatenate([w, q * eg], axis=0).astype(BF16)
                ws = jnp.dot(wq, s.astype(BF16), preferred_element_type=F32)
                v_new = u - ws[:CHUNK]
                vb = v_new.astype(BF16)
                o_ref[rows, l0:l0 + HEAD_DIM] = ws[CHUNK:] + jnp.dot(
                    attn.astype(BF16), vb, preferred_element_type=F32)
                s = s * jnp.exp(glast) + lax.dot_general(
                    kt.astype(BF16), vb, tn, preferred_element_type=F32)
            s_ref[d * GDN_HG + hl] = s


def _gdn_scan(qkv, gates, gates_t):
    ngroups = HEADS // GDN_HG
    hw = GDN_HG * HEAD_DIM

    def blk(p, t, bwd):
        b = p // ngroups
        lat = b * GDN_LAT_BLKS + (GDN_LAT_BLKS - t if bwd else t - 1)
        return jnp.where(t == 0, GDN_CTX_BLK + b, lat)

    def specs(bwd):
        col = lambda base: (lambda p, t: (blk(p, t, bwd), base * ngroups + p % ngroups))
        return [pl.BlockSpec((GDN_TB, hw), col(0)),
                pl.BlockSpec((GDN_TB, hw), col(1)),
                pl.BlockSpec((GDN_TB, hw), col(2)),
                pl.BlockSpec((GDN_TB, hw), col(0)),
                pl.BlockSpec((8 * GDN_HG, GDN_TB), lambda p, t: (p % ngroups, blk(p, t, bwd)))]

    out_spec = lambda bwd: pl.BlockSpec((GDN_TB, hw), lambda p, t: (blk(p, t, bwd), p % ngroups))
    return pl.pallas_call(
        _gdn_scan_kernel,
        grid=(BATCH * ngroups, GDN_STEPS),
        in_specs=specs(False) + specs(True),
        out_specs=[out_spec(False), out_spec(True)],
        out_shape=[jax.ShapeDtypeStruct((ROWS, D), F32)] * 2,
        scratch_shapes=[pltpu.VMEM((2 * GDN_HG, HEAD_DIM, HEAD_DIM), F32)],
        compiler_params=_params(("parallel", "arbitrary")),
        name="gdn_scan",
    )(qkv, qkv, qkv, gates, gates_t, qkv, qkv, qkv, gates, gates_t)


def _gdn_out_kernel(x_ref, of_ref, ob_ref, zg_ref, ng_ref, gate_ref, w_ref, o_ref):
    zg = zg_ref[...]
    y = _head_rms(of_ref[...] + ob_ref[...], ng_ref[...]) * (zg * jax.nn.sigmoid(zg))
    o_ref[...] = x_ref[...] + gate_ref[...] * jnp.dot(
        y.astype(BF16), w_ref[...], preferred_element_type=F32)


def _gdn_out(x, o_f, o_b, z, norm_g, mods, w_out):
    row = pl.BlockSpec((TM, D), lambda i: (i, 0))
    return pl.pallas_call(
        _gdn_out_kernel,
        grid=(N_TILES,),
        in_specs=[row, row, row,
                  pl.BlockSpec((TM, D), lambda i: (i, GDN_GATE_COL)),
                  pl.BlockSpec((1, HEAD_DIM), lambda i: (0, 0)),
                  _mod_spec(2),
                  pl.BlockSpec((D, D), lambda i: (0, 0))],
        out_specs=row,
        out_shape=jax.ShapeDtypeStruct((ROWS, D), F32),
        compiler_params=_params(("parallel",)),
        name="gdn_out",
    )(x, o_f, o_b, z, norm_g.reshape(1, HEAD_DIM), mods, w_out)


def _gdn_layer(x, mods, g1, w_in, conv_w, a_log, dt_bias, norm_g, w_out):
    ab = w_in[:, 4 * D:].reshape(D, 2, 2, HEADS)
    ab = jnp.transpose(ab, (0, 3, 1, 2)).reshape(D, HEADS, 4)
    ab = jnp.pad(ab, ((0, 0), (0, 0), (0, HEAD_DIM - 4))).reshape(D, D)
    w = jnp.concatenate([w_in[:, :4 * D], ab], axis=1).astype(BF16)

    def pad_head_param(p):
        p = jnp.pad(jnp.transpose(p, (1, 0)), ((0, 0), (0, HEAD_DIM - 2)))
        return p.reshape(1, D).astype(F32)

    z = _proj_in(x, g1, mods, w, tn=1024)
    qkv, gates, gates_t = _gdn_prep(z, conv_w, pad_head_param(a_log), pad_head_param(dt_bias))
    o_f, o_b = _gdn_scan(qkv, gates, gates_t)
    return _gdn_out(x, o_f, o_b, z, norm_g, mods, w_out.astype(BF16))


def _sconv_out_kernel(x_ref, bg_ref, cg_ref, hv_ref, cgp_ref, hvp_ref, cgn_ref, hvn_ref,
                      cw_ref, gate_ref, w_ref, o_ref):
    u = _conv3(cg_ref[...] * hv_ref[...], cgp_ref[...] * hvp_ref[...],
               cgn_ref[...] * hvn_ref[...], cw_ref[...], pl.program_id(0) * TM)
    y = bg_ref[...] * u
    o_ref[...] = x_ref[...] + gate_ref[...] * jnp.dot(
        y.astype(BF16), w_ref[...], preferred_element_type=F32)


def _sconv_layer(x, mods, g1, w_in, conv_w, w_out):
    z = _proj_in(x, g1, mods, w_in.astype(BF16), tn=1024)
    colspec = lambda cidx: pl.BlockSpec((TM, D), lambda i: (i, cidx))
    p1, n1 = _halo_specs(TM, D, lambda i: 1)
    p2, n2 = _halo_specs(TM, D, lambda i: 2)
    return pl.pallas_call(
        _sconv_out_kernel,
        grid=(N_TILES,),
        in_specs=[colspec(0), colspec(0), colspec(1), colspec(2), p1, p2, n1, n2,
                  pl.BlockSpec((3, D), lambda i: (0, 0)),
                  _mod_spec(2),
                  pl.BlockSpec((D, D), lambda i: (0, 0))],
        out_specs=colspec(0),
        out_shape=jax.ShapeDtypeStruct((ROWS, D), F32),
        compiler_params=_params(("parallel",)),
        name="sconv_out",
    )(x, z, z, z, z, z, z, z, conv_w, mods, w_out.astype(BF16))


def _rope_tables():
    t = jnp.arange(SEQ)
    row = (t // GRID_W).astype(F32)
    colp = (t % GRID_W).astype(F32)
    nf = DIFF_DH // 4
    inv = ROPE_BASE ** (-jnp.arange(nf, dtype=F32) / nf)
    ang = jnp.concatenate([row[:, None] * inv, colp[:, None] * inv], axis=-1)
    ang = jnp.tile(jnp.repeat(ang, 2, axis=-1), (1, HEAD_DIM // DIFF_DH))
    even = (jnp.arange(HEAD_DIM) % 2) == 0
    cos, sin = jnp.cos(ang), jnp.sin(ang)
    return cos, jnp.where(even, -sin, 0.0), jnp.where(even, 0.0, sin)


def _diff_prep_kernel(z_ref, cos_ref, sa_ref, sb_ref, o_ref):
    i, j = pl.program_id(0), pl.program_id(1)
    scale = jnp.where(j == 0, DIFF_DH ** -0.5, 1.0).astype(F32)

    @pl.when(jnp.logical_and(j < 2, i < N_LAT_TILES))
    def _():
        cos, sa, sb = cos_ref[...], sa_ref[...], sb_ref[...]
        for h in range(HEADS):
            sl = slice(h * HEAD_DIM, (h + 1) * HEAD_DIM)
            x = z_ref[:, sl]
            y = x * cos + pltpu.roll(x, HEAD_DIM - 1, 1) * sa + pltpu.roll(x, 1, 1) * sb
            o_ref[:, sl] = (y * scale).astype(BF16)

    @pl.when(jnp.logical_or(j == 2, i >= N_LAT_TILES))
    def _():
        o_ref[...] = (z_ref[...] * scale).astype(BF16)


def _diff_prep(z):
    tab = pl.BlockSpec((TM, HEAD_DIM), lambda i, j: (jnp.minimum(i, N_LAT_TILES - 1) % TILES_PER_BATCH, 0))
    return pl.pallas_call(
        _diff_prep_kernel,
        grid=(N_TILES, 3),
        in_specs=[pl.BlockSpec((TM, D), lambda i, j: (i, j)), tab, tab, tab],
        out_specs=pl.BlockSpec((TM, D), lambda i, j: (i, j)),
        out_shape=jax.ShapeDtypeStruct((ROWS, 3 * D), BF16),
        compiler_params=_params(("parallel", "parallel")),
        name="diff_prep",
    )(z, *_rope_tables())


def _lambda_full(lam_ref, lambda_init):
    lf = lam_ref[...]
    a = jnp.sum(lf[0:1, :] * lf[1:2, :], axis=-1, keepdims=True)
    b = jnp.sum(lf[2:3, :] * lf[3:4, :], axis=-1, keepdims=True)
    return jnp.exp(a) - jnp.exp(b) + lambda_init


def _two_map_scores(q, k):
    lane = lax.broadcasted_iota(jnp.int32, q.shape, 1)
    nt = (((1,), (1,)), ((), ()))
    zero = jnp.zeros_like(q)
    s1 = lax.dot_general(jnp.where(lane < DIFF_DH, q, zero), k, nt, preferred_element_type=F32)
    s2 = lax.dot_general(jnp.where(lane >= DIFF_DH, q, zero), k, nt, preferred_element_type=F32)
    return s1, s2


def _attn_lat_kernel(lam_ref, q_ref, kc_ref, vc_ref, k_ref, v_ref, o_ref,
                     m_ref, l_ref, acc_ref, *, lambda_init):
    j = pl.program_id(2)

    @pl.when(j == 0)
    def _():
        m_ref[...] = jnp.full_like(m_ref, -jnp.inf)
        l_ref[...] = jnp.zeros_like(l_ref)
        acc_ref[...] = jnp.zeros_like(acc_ref)

    def update(k, v):
        for mi, s in enumerate(_two_map_scores(q_ref[...], k)):
            m_old = m_ref[mi]
            m_new = jnp.maximum(m_old, jnp.max(s, axis=-1, keepdims=True))
            alpha = jnp.exp(m_old - m_new)
            p = jnp.exp(s - m_new)
            l_ref[mi] = alpha * l_ref[mi] + jnp.sum(p, axis=-1, keepdims=True)
            acc_ref[mi] = alpha * acc_ref[mi] + jnp.dot(p.astype(BF16), v, preferred_element_type=F32)
            m_ref[mi] = m_new

    @pl.when(j == 0)
    def _():
        update(kc_ref[...], vc_ref[...])

    update(k_ref[...], v_ref[...])

    @pl.when(j == pl.num_programs(2) - 1)
    def _():
        lmb = _lambda_full(lam_ref, lambda_init)
        o_ref[...] = acc_ref[0] / l_ref[0] - lmb * (acc_ref[1] / l_ref[1])


def _attn_ctx_kernel(lam_ref, q_ref, k_ref, v_ref, o_ref, *, lambda_init):
    outs = []
    for s in _two_map_scores(q_ref[...], k_ref[...]):
        p = jnp.exp(s - jnp.max(s, axis=-1, keepdims=True))
        o = jnp.dot(p.astype(BF16), v_ref[...], preferred_element_type=F32)
        outs.append(o / jnp.sum(p, axis=-1, keepdims=True))
    o_ref[...] = outs[0] - _lambda_full(lam_ref, lambda_init) * outs[1]


ATT_TQ = 1024
ATT_TK = 1024


def _diff_attention(qkv, lam, lambda_init):
    nq, nk = SEQ // ATT_TQ, SEQ // ATT_TK
    ctx_blk = LAT_ROWS // CTX
    lam_spec3 = pl.BlockSpec((4, DIFF_DH), lambda p, i, j: (0, 0))
    o_lat = pl.pallas_call(
        functools.partial(_attn_lat_kernel, lambda_init=lambda_init),
        grid=(BATCH * HEADS, nq, nk),
        in_specs=[lam_spec3,
                  pl.BlockSpec((ATT_TQ, HEAD_DIM), lambda p, i, j: ((p // HEADS) * nq + i, p % HEADS)),
                  pl.BlockSpec((CTX, HEAD_DIM), lambda p, i, j: (ctx_blk + p // HEADS, HEADS + p % HEADS)),
                  pl.BlockSpec((CTX, HEAD_DIM), lambda p, i, j: (ctx_blk + p // HEADS, 2 * HEADS + p % HEADS)),
                  pl.BlockSpec((ATT_TK, HEAD_DIM), lambda p, i, j: ((p // HEADS) * nk + j, HEADS + p % HEADS)),
                  pl.BlockSpec((ATT_TK, HEAD_DIM), lambda p, i, j: ((p // HEADS) * nk + j, 2 * HEADS + p % HEADS))],
        out_specs=pl.BlockSpec((ATT_TQ, HEAD_DIM), lambda p, i, j: ((p // HEADS) * nq + i, p % HEADS)),
        out_shape=jax.ShapeDtypeStruct((LAT_ROWS, D), F32),
        scratch_shapes=[pltpu.VMEM((2, ATT_TQ, 1), F32), pltpu.VMEM((2, ATT_TQ, 1), F32),
                        pltpu.VMEM((2, ATT_TQ, HEAD_DIM), F32)],
        compiler_params=_params(("parallel", "parallel", "arbitrary")),
        name="diff_attn_lat",
    )(lam, qkv, qkv, qkv, qkv, qkv)
    o_ctx = pl.pallas_call(
        functools.partial(_attn_ctx_kernel, lambda_init=lambda_init),
        grid=(BATCH, HEADS),
        in_specs=[pl.BlockSpec((4, DIFF_DH), lambda b, h: (0, 0)),
                  pl.BlockSpec((CTX, HEAD_DIM), lambda b, h: (ctx_blk + b, h)),
                  pl.BlockSpec((CTX, HEAD_DIM), lambda b, h: (ctx_blk + b, HEADS + h)),
                  pl.BlockSpec((CTX, HEAD_DIM), lambda b, h: (ctx_blk + b, 2 * HEADS + h))],
        out_specs=pl.BlockSpec((CTX, HEAD_DIM), lambda b, h: (b, h)),
        out_shape=jax.ShapeDtypeStruct((CTX_ROWS, D), F32),
        compiler_params=_params(("parallel", "parallel")),
        name="diff_attn_ctx",
    )(lam, qkv, qkv, qkv)
    return o_lat, o_ctx


def _diff_out_kernel(x_ref, ol_ref, oc_ref, ng_ref, gate_ref, w_ref, o_ref, *, out_scale):
    is_ctx = pl.program_id(0) >= N_LAT_TILES
    o = jnp.where(is_ctx, oc_ref[...], ol_ref[...])
    y = _head_rms(o, ng_ref[...]) * out_scale
    o_ref[...] = x_ref[...] + gate_ref[...] * jnp.dot(
        y.astype(BF16), w_ref[...], preferred_element_type=F32)


def _diff_layer(x, mods, g1, w_in, lam, norm_g, w_out, lambda_init):
    z = _proj_in(x, g1, mods, w_in.astype(BF16), tn=1024)
    o_lat, o_ctx = _diff_attention(_diff_prep(z), lam, lambda_init)
    row = pl.BlockSpec((TM, D), lambda i: (i, 0))
    return pl.pallas_call(
        functools.partial(_diff_out_kernel, out_scale=1.0 - lambda_init),
        grid=(N_TILES,),
        in_specs=[row,
                  pl.BlockSpec((TM, D), lambda i: (jnp.minimum(i, N_LAT_TILES - 1), 0)),
                  pl.BlockSpec((TM, D), lambda i: (0, 0)),
                  pl.BlockSpec((1, HEAD_DIM), lambda i: (0, 0)),
                  _mod_spec(2),
                  pl.BlockSpec((D, D), lambda i: (0, 0))],
        out_specs=row,
        out_shape=jax.ShapeDtypeStruct((ROWS, D), F32),
        compiler_params=_params(("parallel",)),
        name="diff_out",
    )(x, o_lat, o_ctx, norm_g.reshape(1, HEAD_DIM), mods, w_out.astype(BF16))


def kernel(x, c, ctx, c_ctx, norm1_g, norm2_g, ada_w, ada_b, mlp_w1, mlp_w2, gdn_w_in, gdn_conv, gdn_a_log, gdn_dt_bias, gdn_norm_g, gdn_w_out, sconv_w_in, sconv_conv, sconv_w_out, diff_w_in, diff_lambda, diff_norm_g, diff_w_out, final_g):
    mods = _ada(c, c_ctx, ada_w, ada_b)
    h = jnp.concatenate([x.reshape(LAT_ROWS, D), ctx.reshape(CTX_ROWS, D)], axis=0)
    for i in range(DEPTH):
        kind, j = i % 3, i // 3
        if kind == 0:
            h = _gdn_layer(h, mods[i], norm1_g[i], gdn_w_in[j], gdn_conv[j], gdn_a_log[j],
                           gdn_dt_bias[j], gdn_norm_g[j], gdn_w_out[j])
        elif kind == 1:
            h = _sconv_layer(h, mods[i], norm1_g[i], sconv_w_in[j], sconv_conv[j], sconv_w_out[j])
        else:
            lambda_init = 0.8 - 0.6 * math.exp(-0.3 * i)
            h = _diff_layer(h, mods[i], norm1_g[i], diff_w_in[j], diff_lambda[j],
                            diff_norm_g[j], diff_w_out[j], lambda_init)
        h = _mlp(h, norm2_g[i], mods[i], mlp_w1[i].astype(BF16), mlp_w2[i].astype(BF16),
                 final_g, last=i == DEPTH - 1)
    return h.reshape(BATCH, SEQ, D)
```

```python
import functools
import math

import jax
import jax.numpy as jnp
from jax import lax
from jax.experimental import pallas as pl
from jax.experimental.pallas import tpu as pltpu

F32 = jnp.float32
BF16 = jnp.bfloat16

D = 1024
BATCH = 2
SEQ = 8192
CTX = 256
DEPTH = 4
N_ADA = 6
D_FF = 4 * D
EPS = 1e-6
HEADS = 8
HEAD_DIM = 128
CHUNK = 64
DIFF_DH = 64
GRID_W = 64
ROPE_BASE = 10000.0

LAT_ROWS = BATCH * SEQ
CTX_ROWS = BATCH * CTX
ROWS = LAT_ROWS + CTX_ROWS
TM = 512
N_TILES = ROWS // TM
N_LAT_TILES = LAT_ROWS // TM
TILES_PER_BATCH = SEQ // TM
VMEM_LIMIT = 56 * 1024 * 1024


def _params(sem, vmem=VMEM_LIMIT):
    return pltpu.CompilerParams(dimension_semantics=sem, vmem_limit_bytes=vmem)


def _mod_row(i):
    return jnp.where(i >= N_LAT_TILES, 0, 1 + i // TILES_PER_BATCH)


def _mod_spec(k):
    return pl.BlockSpec((None, None, 1, D), lambda i, *_: (_mod_row(i), k, 0, 0))


def _norm_mod(x, g, sh, sc):
    ms = jnp.mean(x * x, axis=-1, keepdims=True)
    return (x * lax.rsqrt(ms + EPS) * g) * (1.0 + sc) + sh


def _head_rms(o, g):
    outs = []
    for h in range(HEADS):
        oh = o[:, h * HEAD_DIM:(h + 1) * HEAD_DIM]
        ms = jnp.mean(oh * oh, axis=-1, keepdims=True)
        outs.append(oh * lax.rsqrt(ms + EPS) * g)
    return jnp.concatenate(outs, axis=1)


def _seq_edges(r0, tm):
    rows = r0 + lax.broadcasted_iota(jnp.int32, (tm, 1), 0)
    starts = (0, SEQ, LAT_ROWS, LAT_ROWS + CTX)
    is_start = functools.reduce(jnp.logical_or, [rows == s for s in starts])
    is_end = functools.reduce(jnp.logical_or, [rows == s + n - 1 for s, n in
                                               zip(starts, (SEQ, SEQ, CTX, CTX))])
    return is_start, is_end


def _conv3(x, prev8, next8, w, r0):
    tm = x.shape[0]
    loc = lax.broadcasted_iota(jnp.int32, (tm, 1), 0)
    is_start, is_end = _seq_edges(r0, tm)
    x_dn = jnp.where(loc == 0, prev8[7:8, :], pltpu.roll(x, 1, 0))
    x_dn = jnp.where(is_start, 0.0, x_dn)
    x_up = jnp.where(loc == tm - 1, next8[0:1, :], pltpu.roll(x, tm - 1, 0))
    x_up = jnp.where(is_end, 0.0, x_up)
    return x_dn * w[0:1, :] + x * w[1:2, :] + x_up * w[2:3, :]


def _halo_specs(tm, tc, col_of):
    nb = ROWS // 8
    prev = pl.BlockSpec((8, tc), lambda i, *a: (jnp.maximum(i * (tm // 8) - 1, 0), col_of(i, *a)))
    nxt = pl.BlockSpec((8, tc), lambda i, *a: (jnp.minimum((i + 1) * (tm // 8), nb - 1), col_of(i, *a)))
    return prev, nxt


def _ada_kernel(s_ref, w_ref, b_ref, o_ref):
    s = s_ref[...]
    s = s * jax.nn.sigmoid(s)
    o_ref[...] = jnp.dot(s.astype(BF16), w_ref[...].astype(BF16),
                         preferred_element_type=F32) + b_ref[...]


def _ada(c, c_ctx, ada_w, ada_b):
    tn = 1536
    s = jnp.concatenate([c_ctx[None, :], c, jnp.zeros((5, D), F32)], axis=0)
    out = pl.pallas_call(
        _ada_kernel,
        grid=(DEPTH, N_ADA * D // tn),
        in_specs=[pl.BlockSpec((8, D), lambda l, j: (0, 0)),
                  pl.BlockSpec((None, D, tn), lambda l, j: (l, 0, j)),
                  pl.BlockSpec((None, 1, tn), lambda l, j: (l, 0, j))],
        out_specs=pl.BlockSpec((None, 8, tn), lambda l, j: (l, 0, j)),
        out_shape=jax.ShapeDtypeStruct((DEPTH, 8, N_ADA * D), F32),
        compiler_params=_params(("parallel", "parallel")),
        name="ada",
    )(s, ada_w, ada_b.reshape(DEPTH, 1, N_ADA * D))
    return out[:, :3].reshape(DEPTH, 3, N_ADA, 1, D)


def _proj_in_kernel(x_ref, g_ref, sh_ref, sc_ref, w_ref, o_ref, h_ref):
    @pl.when(pl.program_id(1) == 0)
    def _():
        h_ref[...] = _norm_mod(x_ref[...], g_ref[...], sh_ref[...], sc_ref[...]).astype(BF16)

    o_ref[...] = jnp.dot(h_ref[...], w_ref[...], preferred_element_type=F32)


def _proj_in(x, g, mods, w, tn):
    n = w.shape[1]
    return pl.pallas_call(
        _proj_in_kernel,
        grid=(N_TILES, n // tn),
        in_specs=[pl.BlockSpec((TM, D), lambda i, j: (i, 0)),
                  pl.BlockSpec((1, D), lambda i, j: (0, 0)),
                  _mod_spec(0), _mod_spec(1),
                  pl.BlockSpec((D, tn), lambda i, j: (0, j))],
        out_specs=pl.BlockSpec((TM, tn), lambda i, j: (i, j)),
        out_shape=jax.ShapeDtypeStruct((ROWS, n), F32),
        scratch_shapes=[pltpu.VMEM((TM, D), BF16)],
        compiler_params=_params(("parallel", "arbitrary")),
        name="proj_in",
    )(x, g.reshape(1, D), mods, mods, w)


def _mlp_kernel(x_ref, g_ref, sh_ref, sc_ref, gate_ref, w1_ref, w2_ref, fg_ref, o_ref,
                h_ref, acc_ref, *, final_norm):
    j = pl.program_id(1)

    @pl.when(j == 0)
    def _():
        h_ref[...] = _norm_mod(x_ref[...], g_ref[...], sh_ref[...], sc_ref[...]).astype(BF16)
        acc_ref[...] = jnp.zeros_like(acc_ref)

    a = jnp.dot(h_ref[...], w1_ref[...], preferred_element_type=F32)
    a = jnp.square(jnp.maximum(a, 0.0))
    acc_ref[...] += jnp.dot(a.astype(BF16), w2_ref[...], preferred_element_type=F32)

    @pl.when(j == pl.num_programs(1) - 1)
    def _():
        y = x_ref[...] + gate_ref[...] * acc_ref[...]
        if final_norm:
            ms = jnp.mean(y * y, axis=-1, keepdims=True)
            y = y * lax.rsqrt(ms + EPS) * fg_ref[...]
        o_ref[...] = y


def _mlp(x, g, mods, w1, w2, final_g, *, last):
    tf = 1024
    n_tiles = N_LAT_TILES if last else N_TILES
    return pl.pallas_call(
        functools.partial(_mlp_kernel, final_norm=last),
        grid=(n_tiles, D_FF // tf),
        in_specs=[pl.BlockSpec((TM, D), lambda i, j: (i, 0)),
                  pl.BlockSpec((1, D), lambda i, j: (0, 0)),
                  _mod_spec(3), _mod_spec(4), _mod_spec(5),
                  pl.BlockSpec((D, tf), lambda i, j: (0, j)),
                  pl.BlockSpec((tf, D), lambda i, j: (j, 0)),
                  pl.BlockSpec((1, D), lambda i, j: (0, 0))],
        out_specs=pl.BlockSpec((TM, D), lambda i, j: (i, 0)),
        out_shape=jax.ShapeDtypeStruct((n_tiles * TM, D), F32),
        scratch_shapes=[pltpu.VMEM((TM, D), BF16), pltpu.VMEM((TM, D), F32)],
        compiler_params=_params(("parallel", "arbitrary")),
        name="mlp",
    )(x, g.reshape(1, D), mods, mods, mods, w1, w2, final_g.reshape(1, D))


GDN_QKV = 3 * D
GDN_GATE_COL = 3
GDN_AB_COL = 4


def _gdn_prep_kernel(z_ref, zp_ref, zn_ref, cw_ref, ab_ref, alog_ref, dtb_ref,
                     qkv_ref, gates_ref, gates_t_ref):
    i, j = pl.program_id(0), pl.program_id(1)
    y = _conv3(z_ref[...], zp_ref[...], zn_ref[...], cw_ref[...], i * TM)
    y = y * jax.nn.sigmoid(y)

    @pl.when(j < 2)
    def _():
        scale = jnp.where(j == 0, HEAD_DIM ** -0.5, 1.0).astype(F32)
        outs = []
        for h in range(HEADS):
            yh = y[:, h * HEAD_DIM:(h + 1) * HEAD_DIM]
            ss = jnp.sum(yh * yh, axis=-1, keepdims=True)
            outs.append(yh * (lax.rsqrt(ss + EPS) * scale))
        qkv_ref[...] = jnp.concatenate(outs, axis=1)

    @pl.when(j == 2)
    def _():
        qkv_ref[...] = y

    @pl.when(j == 0)
    def _():
        ab = ab_ref[...]
        lane = lax.broadcasted_iota(jnp.int32, ab.shape, 1) & (HEAD_DIM - 1)
        x = ab + dtb_ref[...]
        softplus = jnp.maximum(x, 0.0) + jnp.log1p(jnp.exp(-jnp.abs(x)))
        gates = jnp.where(lane < 2, -jnp.exp(alog_ref[...]) * softplus, jax.nn.sigmoid(ab))
        gates_ref[...] = gates
        for h in range(HEADS):
            gt = gates[:, h * HEAD_DIM:(h + 1) * HEAD_DIM].T
            gates_t_ref[h * 8:(h + 1) * 8, :] = gt[0:8, :]


def _gdn_prep(z, conv_w, alog_pad, dtb_pad):
    col = lambda i, j: j
    prev, nxt = _halo_specs(TM, D, col)
    return pl.pallas_call(
        _gdn_prep_kernel,
        grid=(N_TILES, 3),
        in_specs=[pl.BlockSpec((TM, D), lambda i, j: (i, j)), prev, nxt,
                  pl.BlockSpec((3, D), lambda i, j: (0, j)),
                  pl.BlockSpec((TM, D), lambda i, j: (i, GDN_AB_COL)),
                  pl.BlockSpec((1, D), lambda i, j: (0, 0)),
                  pl.BlockSpec((1, D), lambda i, j: (0, 0))],
        out_specs=[pl.BlockSpec((TM, D), lambda i, j: (i, j)),
                   pl.BlockSpec((TM, D), lambda i, j: (i, 0)),
                   pl.BlockSpec((8 * HEADS, TM), lambda i, j: (0, i))],
        out_shape=[jax.ShapeDtypeStruct((ROWS, GDN_QKV), F32),
                   jax.ShapeDtypeStruct((ROWS, D), F32),
                   jax.ShapeDtypeStruct((8 * HEADS, ROWS), F32)],
        compiler_params=_params(("parallel", "arbitrary")),
        name="gdn_prep",
    )(z, z, z, conv_w, z, alog_pad, dtb_pad)


GDN_TB = 256
GDN_NC = GDN_TB // CHUNK
GDN_HG = 2
GDN_STEPS = (CTX + SEQ) // GDN_TB
GDN_CTX_BLK = LAT_ROWS // GDN_TB
GDN_LAT_BLKS = SEQ // GDN_TB


def _split_dot(a_exact, x, *, x_is_lhs=False):
    hi = x.astype(BF16)
    r1 = x - hi.astype(F32)
    mid = r1.astype(BF16)
    lo = (r1 - mid.astype(F32)).astype(BF16)
    out = None
    for part in (hi, mid, lo):
        t = (jnp.dot(part, a_exact, preferred_element_type=F32) if x_is_lhs
             else jnp.dot(a_exact, part, preferred_element_type=F32))
        out = t if out is None else out + t
    return out


TRI_BASE_SHIFT = 3


def _tri_masks(r, c):
    same = lambda shift: (r >> shift) == (c >> shift)
    offs = [same(shift + 1) & jnp.logical_not(same(shift)) for shift in range(TRI_BASE_SHIFT, 6)]
    return same(TRI_BASE_SHIFT), offs


def _bdot(a, b):
    return jnp.dot(a.astype(BF16), b.astype(BF16), preferred_element_type=F32)


def _tri_inverse_all(lms, eye, masks):
    base, offs = masks
    lds = [jnp.where(base, lm, 0.0) for lm in lms]
    ps = [eye - ld for ld in lds]
    qs = [_bdot(ld, ld) for ld in lds]
    pqs = [_bdot(jnp.concatenate([q, p], axis=0), q) for q, p in zip(qs, ps)]
    ts = [p + pq[CHUNK:] for p, pq in zip(ps, pqs)]
    ts = [t + _bdot(t, pq[:CHUNK]) for t, pq in zip(ts, pqs)]
    for off in offs:
        xs = [_bdot(jnp.where(off, lm, 0.0), t) for lm, t in zip(lms, ts)]
        ts = [t - _bdot(t, x) for t, x in zip(ts, xs)]
    return ts


def _gdn_scan_kernel(qf_ref, kf_ref, vf_ref, gf_ref, gtf_ref,
                     qb_ref, kb_ref, vb_ref, gb_ref, gtb_ref,
                     of_ref, ob_ref, s_ref):
    @pl.when(pl.program_id(1) == 0)
    def _():
        s_ref[...] = jnp.zeros_like(s_ref)

    r = lax.broadcasted_iota(jnp.int32, (CHUNK, CHUNK), 0)
    c = lax.broadcasted_iota(jnp.int32, (CHUNK, CHUNK), 1)
    eye = (r == c).astype(F32)
    tri_masks = _tri_masks(r, c)
    r2 = lax.broadcasted_iota(jnp.int32, (GDN_TB, GDN_TB), 0)
    c2 = lax.broadcasted_iota(jnp.int32, (GDN_TB, GDN_TB), 1)
    same = (r2 >> 6) == (c2 >> 6)
    lower = jnp.where(same & (r2 >= c2), 1.0, 0.0).astype(BF16)
    upper = jnp.where(same & (r2 <= c2), 1.0, 0.0).astype(BF16)
    nt = (((1,), (1,)), ((), ()))
    tn = (((0,), (0,)), ((), ()))

    dirs = ((qf_ref, kf_ref, vf_ref, gf_ref, gtf_ref, of_ref),
            (qb_ref, kb_ref, vb_ref, gb_ref, gtb_ref, ob_ref))
    probs = []
    for d, (q_ref, k_ref, v_ref, g_ref, gt_ref, o_ref) in enumerate(dirs):
        fwd = d == 0
        g_all = g_ref[...]
        gc_cols = _split_dot(lower if fwd else upper, g_all)
        gc_rows = _split_dot(upper if fwd else lower, gt_ref[...], x_is_lhs=True)
        last = CHUNK - 1 if fwd else 0
        for hl in range(GDN_HG):
            l0 = hl * HEAD_DIM
            for ci in range(GDN_NC):
                rows = slice(ci * CHUNK, (ci + 1) * CHUNK)
                gcol = gc_cols[rows, l0 + d:l0 + d + 1]
                probs.append(dict(
                    d=d, hl=hl, ci=ci, rows=rows, l0=l0, o_ref=o_ref,
                    incl=(r >= c) if fwd else (r <= c),
                    strict=(r > c) if fwd else (r < c),
                    q=q_ref[rows, l0:l0 + HEAD_DIM],
                    k=k_ref[rows, l0:l0 + HEAD_DIM],
                    v=v_ref[rows, l0:l0 + HEAD_DIM],
                    gcol=gcol,
                    glast=gcol[last:last + 1, :],
                    beta=g_all[rows, l0 + 2 + d:l0 + 3 + d],
                    grow=gc_rows[8 * hl + d:8 * hl + d + 1, rows]))

    for p in probs:
        p['decay'] = jnp.where(p['incl'], jnp.exp(jnp.where(p['incl'], p['gcol'] - p['grow'], 0.0)), 0.0)
        kq = jnp.concatenate([p['k'], p['q']], axis=0).astype(BF16)
        p['kkqk'] = lax.dot_general(kq, p['k'].astype(BF16), nt, preferred_element_type=F32)
    for p in probs:
        p['lm'] = jnp.where(p['strict'], p['beta'] * p['kkqk'][:CHUNK] * p['decay'], 0.0)
        p['attn'] = jnp.where(p['incl'], p['kkqk'][CHUNK:] * p['decay'], 0.0)
        eg = jnp.exp(p['gcol'])
        p['qd'] = p['q'] * eg
        p['rhs'] = jnp.concatenate([p['v'] * p['beta'], p['k'] * (p['beta'] * eg)], axis=1)
        p['kt'] = p['k'] * jnp.exp(p['glast'] - p['gcol'])
    tinvs = _tri_inverse_all([p['lm'] for p in probs], eye, tri_masks)
    for p, tinv in zip(probs, tinvs):
        p['uw'] = _bdot(tinv, p['rhs']).astype(BF16)
    for p in probs:
        au_aw = jnp.dot(p['attn'].astype(BF16), p['uw'], preferred_element_type=F32)
        ku_kw = lax.dot_general(p['kt'].astype(BF16), p['uw'], tn, preferred_element_type=F32)
        p['o0'] = au_aw[:, :HEAD_DIM]
        p['n'] = ku_kw[:, :HEAD_DIM]
        p['kq'] = jnp.concatenate([ku_kw[:, HEAD_DIM:], p['qd'] - au_aw[:, HEAD_DIM:]], axis=0).astype(BF16)

    states = [s_ref[i] for i in range(2 * GDN_HG)]
    for step in range(GDN_NC):
        for p in probs:
            if p['ci'] != (step if p['d'] == 0 else GDN_NC - 1 - step):
                continue
            si = p['d'] * GDN_HG + p['hl']
            ks_qs = jnp.dot(p['kq'], states[si].astype(BF16), preferred_element_type=F32)
            p['o_ref'][p['rows'], p['l0']:p['l0'] + HEAD_DIM] = ks_qs[HEAD_DIM:] + p['o0']
            states[si] = states[si] * jnp.exp(p['glast']) - ks_qs[:HEAD_DIM] + p['n']
    for i in range(2 * GDN_HG):
        s_ref[i] = states[i]


def _gdn_scan(qkv, gates, gates_t):
    ngroups = HEADS // GDN_HG
    hw = GDN_HG * HEAD_DIM

    def blk(p, t, bwd):
        b = p // ngroups
        lat = b * GDN_LAT_BLKS + (GDN_LAT_BLKS - t if bwd else t - 1)
        return jnp.where(t == 0, GDN_CTX_BLK + b, lat)

    def specs(bwd):
        col = lambda base: (lambda p, t: (blk(p, t, bwd), base * ngroups + p % ngroups))
        return [pl.BlockSpec((GDN_TB, hw), col(0)),
                pl.BlockSpec((GDN_TB, hw), col(1)),
                pl.BlockSpec((GDN_TB, hw), col(2)),
                pl.BlockSpec((GDN_TB, hw), col(0)),
                pl.BlockSpec((8 * GDN_HG, GDN_TB), lambda p, t: (p % ngroups, blk(p, t, bwd)))]

    out_spec = lambda bwd: pl.BlockSpec((GDN_TB, hw), lambda p, t: (blk(p, t, bwd), p % ngroups))
    return pl.pallas_call(
        _gdn_scan_kernel,
        grid=(BATCH * ngroups, GDN_STEPS),
        in_specs=specs(False) + specs(True),
        out_specs=[out_spec(False), out_spec(True)],
        out_shape=[jax.ShapeDtypeStruct((ROWS, D), F32)] * 2,
        scratch_shapes=[pltpu.VMEM((2 * GDN_HG, HEAD_DIM, HEAD_DIM), F32)],
        compiler_params=_params(("parallel", "arbitrary")),
        name="gdn_scan",
    )(qkv, qkv, qkv, gates, gates_t, qkv, qkv, qkv, gates, gates_t)


def _gdn_out_kernel(x_ref, of_ref, ob_ref, zg_ref, ng_ref, gate_ref, w_ref, o_ref):
    zg = zg_ref[...]
    y = _head_rms(of_ref[...] + ob_ref[...], ng_ref[...]) * (zg * jax.nn.sigmoid(zg))
    o_ref[...] = x_ref[...] + gate_ref[...] * jnp.dot(
        y.astype(BF16), w_ref[...], preferred_element_type=F32)


def _gdn_out(x, o_f, o_b, z, norm_g, mods, w_out):
    row = pl.BlockSpec((TM, D), lambda i: (i, 0))
    return pl.pallas_call(
        _gdn_out_kernel,
        grid=(N_TILES,),
        in_specs=[row, row, row,
                  pl.BlockSpec((TM, D), lambda i: (i, GDN_GATE_COL)),
                  pl.BlockSpec((1, HEAD_DIM), lambda i: (0, 0)),
                  _mod_spec(2),
                  pl.BlockSpec((D, D), lambda i: (0, 0))],
        out_specs=row,
        out_shape=jax.ShapeDtypeStruct((ROWS, D), F32),
        compiler_params=_params(("parallel",)),
        name="gdn_out",
    )(x, o_f, o_b, z, norm_g.reshape(1, HEAD_DIM), mods, w_out)


def _gdn_layer(x, mods, g1, w_in, conv_w, a_log, dt_bias, norm_g, w_out):
    ab = w_in[:, 4 * D:].reshape(D, 2, 2, HEADS)
    ab = jnp.transpose(ab, (0, 3, 1, 2)).reshape(D, HEADS, 4)
    ab = jnp.pad(ab, ((0, 0), (0, 0), (0, HEAD_DIM - 4))).reshape(D, D)
    w = jnp.concatenate([w_in[:, :4 * D], ab], axis=1).astype(BF16)

    def pad_head_param(p):
        p = jnp.pad(jnp.transpose(p, (1, 0)), ((0, 0), (0, HEAD_DIM - 2)))
        return p.reshape(1, D).astype(F32)

    z = _proj_in(x, g1, mods, w, tn=1024)
    qkv, gates, gates_t = _gdn_prep(z, conv_w, pad_head_param(a_log), pad_head_param(dt_bias))
    o_f, o_b = _gdn_scan(qkv, gates, gates_t)
    return _gdn_out(x, o_f, o_b, z, norm_g, mods, w_out.astype(BF16))


def _sconv_out_kernel(x_ref, bg_ref, cg_ref, hv_ref, cgp_ref, hvp_ref, cgn_ref, hvn_ref,
                      cw_ref, gate_ref, w_ref, o_ref):
    u = _conv3(cg_ref[...] * hv_ref[...], cgp_ref[...] * hvp_ref[...],
               cgn_ref[...] * hvn_ref[...], cw_ref[...], pl.program_id(0) * TM)
    y = bg_ref[...] * u
    o_ref[...] = x_ref[...] + gate_ref[...] * jnp.dot(
        y.astype(BF16), w_ref[...], preferred_element_type=F32)


def _sconv_layer(x, mods, g1, w_in, conv_w, w_out):
    z = _proj_in(x, g1, mods, w_in.astype(BF16), tn=1024)
    colspec = lambda cidx: pl.BlockSpec((TM, D), lambda i: (i, cidx))
    p1, n1 = _halo_specs(TM, D, lambda i: 1)
    p2, n2 = _halo_specs(TM, D, lambda i: 2)
    return pl.pallas_call(
        _sconv_out_kernel,
        grid=(N_TILES,),
        in_specs=[colspec(0), colspec(0), colspec(1), colspec(2), p1, p2, n1, n2,
                  pl.BlockSpec((3, D), lambda i: (0, 0)),
                  _mod_spec(2),
                  pl.BlockSpec((D, D), lambda i: (0, 0))],
        out_specs=colspec(0),
        out_shape=jax.ShapeDtypeStruct((ROWS, D), F32),
        compiler_params=_params(("parallel",)),
        name="sconv_out",
    )(x, z, z, z, z, z, z, z, conv_w, mods, w_out.astype(BF16))


def _rope_tables():
    t = jnp.arange(SEQ)
    row = (t // GRID_W).astype(F32)
    colp = (t % GRID_W).astype(F32)
    nf = DIFF_DH // 4
    inv = ROPE_BASE ** (-jnp.arange(nf, dtype=F32) / nf)
    ang = jnp.concatenate([row[:, None] * inv, colp[:, None] * inv], axis=-1)
    ang = jnp.tile(jnp.repeat(ang, 2, axis=-1), (1, HEAD_DIM // DIFF_DH))
    even = (jnp.arange(HEAD_DIM) % 2) == 0
    cos, sin = jnp.cos(ang), jnp.sin(ang)
    return cos, jnp.where(even, -sin, 0.0), jnp.where(even, 0.0, sin)


Q_SCALE = DIFF_DH ** -0.5 * math.log2(math.e)


def _diff_prep_kernel(z_ref, cos_ref, sa_ref, sb_ref, qk_ref, vt_ref):
    i, j = pl.program_id(0), pl.program_id(1)
    scale = jnp.where(j == 0, Q_SCALE, 1.0).astype(F32)

    @pl.when(jnp.logical_and(j < 2, i < N_LAT_TILES))
    def _():
        cos, sa, sb = cos_ref[...], sa_ref[...], sb_ref[...]
        for h in range(HEADS):
            sl = slice(h * HEAD_DIM, (h + 1) * HEAD_DIM)
            x = z_ref[:, sl]
            y = x * cos + pltpu.roll(x, HEAD_DIM - 1, 1) * sa + pltpu.roll(x, 1, 1) * sb
            qk_ref[:, sl] = (y * scale).astype(BF16)

    @pl.when(jnp.logical_and(j < 2, i >= N_LAT_TILES))
    def _():
        qk_ref[...] = (z_ref[...] * scale).astype(BF16)

    @pl.when(j == 2)
    def _():
        vt_ref[...] = z_ref[...].T.astype(BF16)


def _diff_prep(z):
    tab = pl.BlockSpec((TM, HEAD_DIM), lambda i, j: (jnp.minimum(i, N_LAT_TILES - 1) % TILES_PER_BATCH, 0))
    return pl.pallas_call(
        _diff_prep_kernel,
        grid=(N_TILES, 3),
        in_specs=[pl.BlockSpec((TM, D), lambda i, j: (i, j)), tab, tab, tab],
        out_specs=[pl.BlockSpec((TM, D), lambda i, j: (i, jnp.minimum(j, 1))),
                   pl.BlockSpec((D, TM), lambda i, j: (0, i))],
        out_shape=[jax.ShapeDtypeStruct((ROWS, 2 * D), BF16),
                   jax.ShapeDtypeStruct((D, ROWS), BF16)],
        compiler_params=_params(("parallel", "arbitrary")),
        name="diff_prep",
    )(z, *_rope_tables())


def _lambda_full(lam_ref, lambda_init):
    lf = lam_ref[...]
    a = jnp.sum(lf[0:1, :] * lf[1:2, :], axis=-1, keepdims=True)
    b = jnp.sum(lf[2:3, :] * lf[3:4, :], axis=-1, keepdims=True)
    return jnp.exp(a) - jnp.exp(b) + lambda_init


ATT_TQ = 1024
ATT_TK = 1024
ATT_QW = 512


def _split_maps(q):
    lane = lax.broadcasted_iota(jnp.int32, q.shape, 1)
    zero = jnp.zeros_like(q)
    return jnp.where(lane < DIFF_DH, q, zero), jnp.where(lane >= DIFF_DH, q, zero)


def _attn_update(k, vt, qz_ref, m_ref, l_ref, acc_ref):
    nt = (((1,), (1,)), ((), ()))
    chains = [(mi, slice(w * ATT_QW, (w + 1) * ATT_QW))
              for mi in range(2) for w in range(ATT_TQ // ATT_QW)]
    ss = [lax.dot_general(k, qz_ref[mi, sl, :], nt, preferred_element_type=F32) for mi, sl in chains]
    m_olds = [m_ref[mi, :, sl] for mi, sl in chains]
    m_news = [jnp.maximum(mo, jnp.max(s, axis=0, keepdims=True)) for mo, s in zip(m_olds, ss)]
    ps = [jnp.exp2(s - mn) for s, mn in zip(ss, m_news)]
    alphas = [jnp.exp2(mo - mn) for mo, mn in zip(m_olds, m_news)]
    for (mi, sl), p, al, mn in zip(chains, ps, alphas, m_news):
        l_ref[mi, :, sl] = al * l_ref[mi, :, sl] + jnp.sum(p, axis=0, keepdims=True)
        m_ref[mi, :, sl] = mn
    pvs = [jnp.dot(vt, p.astype(BF16), preferred_element_type=F32) for p in ps]
    for (mi, sl), pv, al in zip(chains, pvs, alphas):
        acc_ref[mi, :, sl] = al * acc_ref[mi, :, sl] + pv


def _attn_lat_kernel(lam_ref, q_ref, kc_ref, vct_ref, k_ref, vt_ref, o_ref,
                     qz_ref, m_ref, l_ref, acc_ref, *, lambda_init):
    j = pl.program_id(2)

    @pl.when(j == 0)
    def _():
        q1, q2 = _split_maps(q_ref[...])
        qz_ref[0] = q1
        qz_ref[1] = q2
        m_ref[...] = jnp.full_like(m_ref, -jnp.inf)
        l_ref[...] = jnp.zeros_like(l_ref)
        acc_ref[...] = jnp.zeros_like(acc_ref)
        _attn_update(kc_ref[...], vct_ref[...], qz_ref, m_ref, l_ref, acc_ref)

    _attn_update(k_ref[...], vt_ref[...], qz_ref, m_ref, l_ref, acc_ref)

    @pl.when(j == pl.num_programs(2) - 1)
    def _():
        lmb = _lambda_full(lam_ref, lambda_init)
        ot = acc_ref[0] / l_ref[0] - lmb * (acc_ref[1] / l_ref[1])
        o_ref[...] = ot.T


def _attn_ctx_kernel(lam_ref, q_ref, k_ref, vt_ref, o_ref, *, lambda_init):
    nt = (((1,), (1,)), ((), ()))
    outs = []
    for qz in _split_maps(q_ref[...]):
        s = lax.dot_general(k_ref[...], qz, nt, preferred_element_type=F32)
        p = jnp.exp2(s - jnp.max(s, axis=0, keepdims=True))
        o = jnp.dot(vt_ref[...], p.astype(BF16), preferred_element_type=F32)
        outs.append(o / jnp.sum(p, axis=0, keepdims=True))
    o_ref[...] = (outs[0] - _lambda_full(lam_ref, lambda_init) * outs[1]).T


def _diff_attention(qk, vt, lam, lambda_init):
    nq, nk = SEQ // ATT_TQ, SEQ // ATT_TK
    ctx_blk = LAT_ROWS // CTX
    lam_spec3 = pl.BlockSpec((4, DIFF_DH), lambda p, i, j: (0, 0))
    o_lat = pl.pallas_call(
        functools.partial(_attn_lat_kernel, lambda_init=lambda_init),
        grid=(BATCH * HEADS, nq, nk),
        in_specs=[lam_spec3,
                  pl.BlockSpec((ATT_TQ, HEAD_DIM), lambda p, i, j: ((p // HEADS) * nq + i, p % HEADS)),
                  pl.BlockSpec((CTX, HEAD_DIM), lambda p, i, j: (ctx_blk + p // HEADS, HEADS + p % HEADS)),
                  pl.BlockSpec((HEAD_DIM, CTX), lambda p, i, j: (p % HEADS, ctx_blk + p // HEADS)),
                  pl.BlockSpec((ATT_TK, HEAD_DIM), lambda p, i, j: ((p // HEADS) * nk + j, HEADS + p % HEADS)),
                  pl.BlockSpec((HEAD_DIM, ATT_TK), lambda p, i, j: (p % HEADS, (p // HEADS) * nk + j))],
        out_specs=pl.BlockSpec((ATT_TQ, HEAD_DIM), lambda p, i, j: ((p // HEADS) * nq + i, p % HEADS)),
        out_shape=jax.ShapeDtypeStruct((LAT_ROWS, D), F32),
        scratch_shapes=[pltpu.VMEM((2, ATT_TQ, HEAD_DIM), BF16),
                        pltpu.VMEM((2, 1, ATT_TQ), F32), pltpu.VMEM((2, 1, ATT_TQ), F32),
                        pltpu.VMEM((2, HEAD_DIM, ATT_TQ), F32)],
        compiler_params=_params(("parallel", "parallel", "arbitrary")),
        name="diff_attn_lat",
    )(lam, qk, qk, vt, qk, vt)
    o_ctx = pl.pallas_call(
        functools.partial(_attn_ctx_kernel, lambda_init=lambda_init),
        grid=(BATCH, HEADS),
        in_specs=[pl.BlockSpec((4, DIFF_DH), lambda b, h: (0, 0)),
                  pl.BlockSpec((CTX, HEAD_DIM), lambda b, h: (ctx_blk + b, h)),
                  pl.BlockSpec((CTX, HEAD_DIM), lambda b, h: (ctx_blk + b, HEADS + h)),
                  pl.BlockSpec((HEAD_DIM, CTX), lambda b, h: (h, ctx_blk + b))],
        out_specs=pl.BlockSpec((CTX, HEAD_DIM), lambda b, h: (b, h)),
        out_shape=jax.ShapeDtypeStruct((CTX_ROWS, D), F32),
        compiler_params=_params(("parallel", "parallel")),
        name="diff_attn_ctx",
    )(lam, qk, qk, vt)
    return o_lat, o_ctx


def _diff_out_kernel(x_ref, ol_ref, oc_ref, ng_ref, gate_ref, w_ref, o_ref, *, out_scale):
    is_ctx = pl.program_id(0) >= N_LAT_TILES
    o = jnp.where(is_ctx, oc_ref[...], ol_ref[...])
    y = _head_rms(o, ng_ref[...]) * out_scale
    o_ref[...] = x_ref[...] + gate_ref[...] * jnp.dot(
        y.astype(BF16), w_ref[...], preferred_element_type=F32)


def _diff_layer(x, mods, g1, w_in, lam, norm_g, w_out, lambda_init):
    z = _proj_in(x, g1, mods, w_in.astype(BF16), tn=1024)
    o_lat, o_ctx = _diff_attention(*_diff_prep(z), lam, lambda_init)
    row = pl.BlockSpec((TM, D), lambda i: (i, 0))
    return pl.pallas_call(
        functools.partial(_diff_out_kernel, out_scale=1.0 - lambda_init),
        grid=(N_TILES,),
        in_specs=[row,
                  pl.BlockSpec((TM, D), lambda i: (jnp.minimum(i, N_LAT_TILES - 1), 0)),
                  pl.BlockSpec((TM, D), lambda i: (0, 0)),
                  pl.BlockSpec((1, HEAD_DIM), lambda i: (0, 0)),
                  _mod_spec(2),
                  pl.BlockSpec((D, D), lambda i: (0, 0))],
        out_specs=row,
        out_shape=jax.ShapeDtypeStruct((ROWS, D), F32),
        compiler_params=_params(("parallel",)),
        name="diff_out",
    )(x, o_lat, o_ctx, norm_g.reshape(1, HEAD_DIM), mods, w_out.astype(BF16))


def kernel(x, c, ctx, c_ctx, norm1_g, norm2_g, ada_w, ada_b, mlp_w1, mlp_w2, gdn_w_in, gdn_conv, gdn_a_log, gdn_dt_bias, gdn_norm_g, gdn_w_out, sconv_w_in, sconv_conv, sconv_w_out, diff_w_in, diff_lambda, diff_norm_g, diff_w_out, final_g):
    mods = _ada(c, c_ctx, ada_w, ada_b)
    h = jnp.concatenate([x.reshape(LAT_ROWS, D), ctx.reshape(CTX_ROWS, D)], axis=0)
    for i in range(DEPTH):
        kind, j = i % 3, i // 3
        if kind == 0:
            h = _gdn_layer(h, mods[i], norm1_g[i], gdn_w_in[j], gdn_conv[j], gdn_a_log[j],
                           gdn_dt_bias[j], gdn_norm_g[j], gdn_w_out[j])
        elif kind == 1:
            h = _sconv_layer(h, mods[i], norm1_g[i], sconv_w_in[j], sconv_conv[j], sconv_w_out[j])
        else:
            lambda_init = 0.8 - 0.6 * math.exp(-0.3 * i)
            h = _diff_layer(h, mods[i], norm1_g[i], diff_w_in[j], diff_lambda[j],
                            diff_norm_g[j], diff_w_out[j], lambda_init)
        h = _mlp(h, norm2_g[i], mods[i], mlp_w1[i].astype(BF16), mlp_w2[i].astype(BF16),
                 final_g, last=i == DEPTH - 1)
    return h.reshape(BATCH, SEQ, D)
```

```python
import functools
import math

import jax
import jax.numpy as jnp
from jax import lax
from jax.experimental import pallas as pl
from jax.experimental.pallas import tpu as pltpu

F32 = jnp.float32
BF16 = jnp.bfloat16

D = 1024
BATCH = 2
SEQ = 8192
CTX = 256
DEPTH = 4
N_ADA = 6
D_FF = 4 * D
EPS = 1e-6
HEADS = 8
HEAD_DIM = 128
CHUNK = 64
DIFF_DH = 64
GRID_W = 64
ROPE_BASE = 10000.0

LAT_ROWS = BATCH * SEQ
CTX_ROWS = BATCH * CTX
ROWS = LAT_ROWS + CTX_ROWS
TM = 512
N_TILES = ROWS // TM
N_LAT_TILES = LAT_ROWS // TM
TILES_PER_BATCH = SEQ // TM
VMEM_LIMIT = 56 * 1024 * 1024


def _params(sem, vmem=VMEM_LIMIT):
    return pltpu.CompilerParams(dimension_semantics=sem, vmem_limit_bytes=vmem)


def _mod_row(i):
    return jnp.where(i >= N_LAT_TILES, 0, 1 + i // TILES_PER_BATCH)


def _mod_spec(k):
    return pl.BlockSpec((None, None, 1, D), lambda i, *_: (_mod_row(i), k, 0, 0))


def _norm_mod(x, g, sh, sc):
    ms = jnp.mean(x * x, axis=-1, keepdims=True)
    return (x * lax.rsqrt(ms + EPS) * g) * (1.0 + sc) + sh


def _head_rms(o, g):
    outs = []
    for h in range(HEADS):
        oh = o[:, h * HEAD_DIM:(h + 1) * HEAD_DIM]
        ms = jnp.mean(oh * oh, axis=-1, keepdims=True)
        outs.append(oh * lax.rsqrt(ms + EPS) * g)
    return jnp.concatenate(outs, axis=1)


HALO = 8
BF16_HALO = 16


def _is_seq_start(r):
    starts = (0, SEQ, LAT_ROWS, LAT_ROWS + CTX)
    return functools.reduce(jnp.logical_or, [r == s for s in starts])


def _zero_row(a, row, cond):
    s0 = row // 8 * 8
    sub = lax.broadcasted_iota(jnp.int32, (8, 1), 0)
    slab = jnp.where(jnp.logical_and(sub == row - s0, cond), 0.0, a[s0:s0 + 8])
    parts = [a[:s0]] * (s0 > 0) + [slab] + [a[s0 + 8:]] * (s0 + 8 < a.shape[0])
    return jnp.concatenate(parts, axis=0)


def _conv3(xh, w, r0):
    n = xh.shape[0]
    tm = n - 2 * HALO
    x_dn = pltpu.roll(xh, 1, 0)[HALO:HALO + tm]
    x_up = pltpu.roll(xh, n - 1, 0)[HALO:HALO + tm]
    for c in range(0, tm, CTX):
        nxt = r0 + c + CTX
        x_dn = _zero_row(x_dn, c, _is_seq_start(r0 + c))
        x_up = _zero_row(x_up, c + CTX - 1, jnp.logical_or(_is_seq_start(nxt), nxt == ROWS))
    return x_dn * w[0:1, :] + xh[HALO:HALO + tm] * w[1:2, :] + x_up * w[2:3, :]


def _halo_specs(tm, tc, col_of, rows):
    nb = ROWS // rows
    prev = pl.BlockSpec((rows, tc), lambda i, *a: (jnp.maximum(i * (tm // rows) - 1, 0), col_of(i, *a)))
    nxt = pl.BlockSpec((rows, tc), lambda i, *a: (jnp.minimum((i + 1) * (tm // rows), nb - 1), col_of(i, *a)))
    return prev, nxt


def _ada_kernel(s_ref, w_ref, b_ref, o_ref):
    s = s_ref[...]
    s = s * jax.nn.sigmoid(s)
    o_ref[...] = jnp.dot(s.astype(BF16), w_ref[...].astype(BF16),
                         preferred_element_type=F32) + b_ref[...]


def _ada(c, c_ctx, ada_w, ada_b):
    tn = 1536
    s = jnp.concatenate([c_ctx[None, :], c, jnp.zeros((5, D), F32)], axis=0)
    out = pl.pallas_call(
        _ada_kernel,
        grid=(DEPTH, N_ADA * D // tn),
        in_specs=[pl.BlockSpec((8, D), lambda l, j: (0, 0)),
                  pl.BlockSpec((None, D, tn), lambda l, j: (l, 0, j)),
                  pl.BlockSpec((None, 1, tn), lambda l, j: (l, 0, j))],
        out_specs=pl.BlockSpec((None, 8, tn), lambda l, j: (l, 0, j)),
        out_shape=jax.ShapeDtypeStruct((DEPTH, 8, N_ADA * D), F32),
        compiler_params=_params(("parallel", "parallel")),
        name="ada",
    )(s, ada_w, ada_b.reshape(DEPTH, 1, N_ADA * D))
    return out[:, :3].reshape(DEPTH, 3, N_ADA, 1, D)


def _proj_in_kernel(x_ref, g_ref, sh_ref, sc_ref, w_ref, o_ref, h_ref):
    @pl.when(pl.program_id(1) == 0)
    def _():
        h_ref[...] = _norm_mod(x_ref[...], g_ref[...], sh_ref[...], sc_ref[...]).astype(BF16)

    o_ref[...] = jnp.dot(h_ref[...], w_ref[...], preferred_element_type=F32).astype(BF16)


def _proj_in(x, g, mods, w, tn):
    n = w.shape[1]
    return pl.pallas_call(
        _proj_in_kernel,
        grid=(N_TILES, n // tn),
        in_specs=[pl.BlockSpec((TM, D), lambda i, j: (i, 0)),
                  pl.BlockSpec((1, D), lambda i, j: (0, 0)),
                  _mod_spec(0), _mod_spec(1),
                  pl.BlockSpec((D, tn), lambda i, j: (0, j))],
        out_specs=pl.BlockSpec((TM, tn), lambda i, j: (i, j)),
        out_shape=jax.ShapeDtypeStruct((ROWS, n), BF16),
        scratch_shapes=[pltpu.VMEM((TM, D), BF16)],
        compiler_params=_params(("parallel", "arbitrary")),
        name="proj_in",
    )(x, g.reshape(1, D), mods, mods, w)


def _mlp_kernel(x_ref, g_ref, sh_ref, sc_ref, gate_ref, w1_ref, w2_ref, fg_ref, o_ref,
                h_ref, acc_ref, *, final_norm):
    j = pl.program_id(1)

    @pl.when(j == 0)
    def _():
        h_ref[...] = _norm_mod(x_ref[...], g_ref[...], sh_ref[...], sc_ref[...]).astype(BF16)
        acc_ref[...] = jnp.zeros_like(acc_ref)

    a = jnp.dot(h_ref[...], w1_ref[...], preferred_element_type=F32)
    a = jnp.square(jnp.maximum(a, 0.0))
    acc_ref[...] += jnp.dot(a.astype(BF16), w2_ref[...], preferred_element_type=F32)

    @pl.when(j == pl.num_programs(1) - 1)
    def _():
        y = x_ref[...] + gate_ref[...] * acc_ref[...]
        if final_norm:
            ms = jnp.mean(y * y, axis=-1, keepdims=True)
            y = y * lax.rsqrt(ms + EPS) * fg_ref[...]
        o_ref[...] = y


def _mlp(x, g, mods, w1, w2, final_g, *, last):
    tf = 1024
    n_tiles = N_LAT_TILES if last else N_TILES
    return pl.pallas_call(
        functools.partial(_mlp_kernel, final_norm=last),
        grid=(n_tiles, D_FF // tf),
        in_specs=[pl.BlockSpec((TM, D), lambda i, j: (i, 0)),
                  pl.BlockSpec((1, D), lambda i, j: (0, 0)),
                  _mod_spec(3), _mod_spec(4), _mod_spec(5),
                  pl.BlockSpec((D, tf), lambda i, j: (0, j)),
                  pl.BlockSpec((tf, D), lambda i, j: (j, 0)),
                  pl.BlockSpec((1, D), lambda i, j: (0, 0))],
        out_specs=pl.BlockSpec((TM, D), lambda i, j: (i, 0)),
        out_shape=jax.ShapeDtypeStruct((n_tiles * TM, D), F32),
        scratch_shapes=[pltpu.VMEM((TM, D), BF16), pltpu.VMEM((TM, D), F32)],
        compiler_params=_params(("parallel", "arbitrary")),
        name="mlp",
    )(x, g.reshape(1, D), mods, mods, mods, w1, w2, final_g.reshape(1, D))


GDN_HG = 2
GDN_GROUPS = HEADS // GDN_HG
GDN_GATE_COL = 3
GDN_GL = 4 * GDN_HG
GDN_GW = GDN_GROUPS * HEAD_DIM


def _gdn_proj_kernel(x_ref, xp_ref, xn_ref, g_ref, sh_ref, sc_ref, w_ref, wab_ref, cw_ref,
                     alog_ref, dtb_ref, o_ref, gates_ref, gates_t_ref, h_ref):
    i, j = pl.program_id(0), pl.program_id(1)
    rows = slice(HALO, HALO + TM)

    @pl.when(j == 0)
    def _():
        x = jnp.concatenate([xp_ref[...], x_ref[...], xn_ref[...]], axis=0)
        h_ref[...] = _norm_mod(x, g_ref[...], sh_ref[...], sc_ref[...]).astype(BF16)
        ab = jnp.dot(h_ref[...], wab_ref[...], preferred_element_type=F32)[rows]
        lane = lax.broadcasted_iota(jnp.int32, ab.shape, 1) & 3
        t = ab + dtb_ref[...]
        softplus = jnp.maximum(t, 0.0) + jnp.log1p(jnp.exp(-jnp.abs(t)))
        gates = jnp.where(lane < 2, -jnp.exp(alog_ref[...]) * softplus, jax.nn.sigmoid(ab))
        gates_ref[...] = gates
        for hg in range(GDN_GROUPS):
            gt = gates[:, hg * HEAD_DIM:(hg + 1) * HEAD_DIM].T
            gates_t_ref[hg * GDN_GL:(hg + 1) * GDN_GL, :] = gt[0:GDN_GL, :]

    z = jnp.dot(h_ref[...], w_ref[...], preferred_element_type=F32)

    @pl.when(j < 3)
    def _():
        y = _conv3(z, cw_ref[...], i * TM)
        y = y * jax.nn.sigmoid(y)

        @pl.when(j < 2)
        def _():
            scale = jnp.where(j == 0, HEAD_DIM ** -0.5, 1.0).astype(F32)
            for h in range(HEADS):
                sl = slice(h * HEAD_DIM, (h + 1) * HEAD_DIM)
                yh = y[:, sl]
                ss = jnp.sum(yh * yh, axis=-1, keepdims=True)
                o_ref[:, sl] = (yh * (lax.rsqrt(ss + EPS) * scale)).astype(BF16)

        @pl.when(j == 2)
        def _():
            o_ref[...] = y.astype(BF16)

    @pl.when(j == 3)
    def _():
        o_ref[...] = z[rows].astype(BF16)


def _gdn_proj(x, g, mods, w, wab, conv_w, alog_pad, dtb_pad):
    prev, nxt = _halo_specs(TM, D, lambda i, j: 0, HALO)
    full = lambda shape: pl.BlockSpec(shape, lambda i, j: (0, 0))
    return pl.pallas_call(
        _gdn_proj_kernel,
        grid=(N_TILES, 4),
        in_specs=[pl.BlockSpec((TM, D), lambda i, j: (i, 0)), prev, nxt,
                  full((1, D)), _mod_spec(0), _mod_spec(1),
                  pl.BlockSpec((D, D), lambda i, j: (0, j)),
                  full((D, GDN_GW)),
                  pl.BlockSpec((3, D), lambda i, j: (0, jnp.minimum(j, 2))),
                  full((1, GDN_GW)), full((1, GDN_GW))],
        out_specs=[pl.BlockSpec((TM, D), lambda i, j: (i, j)),
                   pl.BlockSpec((TM, GDN_GW), lambda i, j: (i, 0)),
                   pl.BlockSpec((GDN_GROUPS * GDN_GL, TM), lambda i, j: (0, i))],
        out_shape=[jax.ShapeDtypeStruct((ROWS, 4 * D), BF16),
                   jax.ShapeDtypeStruct((ROWS, GDN_GW), F32),
                   jax.ShapeDtypeStruct((GDN_GROUPS * GDN_GL, ROWS), F32)],
        scratch_shapes=[pltpu.VMEM((TM + 2 * HALO, D), BF16)],
        compiler_params=_params(("parallel", "arbitrary")),
        name="gdn_proj",
    )(x, x, x, g.reshape(1, D), mods, mods, w, wab, conv_w, alog_pad, dtb_pad)


GDN_TB = 256
GDN_NC = GDN_TB // CHUNK
GDN_STEPS = (CTX + SEQ) // GDN_TB
GDN_CTX_BLK = LAT_ROWS // GDN_TB
GDN_LAT_BLKS = SEQ // GDN_TB


def _split_dot(a_exact, x, *, x_is_lhs=False):
    hi = x.astype(BF16)
    r1 = x - hi.astype(F32)
    mid = r1.astype(BF16)
    lo = (r1 - mid.astype(F32)).astype(BF16)
    out = None
    for part in (hi, mid, lo):
        t = (jnp.dot(part, a_exact, preferred_element_type=F32) if x_is_lhs
             else jnp.dot(a_exact, part, preferred_element_type=F32))
        out = t if out is None else out + t
    return out


TRI_BASE_SHIFT = 3


def _tri_masks(r, c):
    same = lambda shift: (r >> shift) == (c >> shift)
    offs = [same(shift + 1) & jnp.logical_not(same(shift)) for shift in range(TRI_BASE_SHIFT, 6)]
    return same(TRI_BASE_SHIFT), offs


def _bdot(a, b):
    return jnp.dot(a.astype(BF16), b.astype(BF16), preferred_element_type=F32)


def _tri_inverse_all(lms, eye, masks):
    base, offs = masks
    lds = [jnp.where(base, lm, 0.0) for lm in lms]
    ps = [eye - ld for ld in lds]
    qs = [_bdot(ld, ld) for ld in lds]
    pqs = [_bdot(jnp.concatenate([q, p], axis=0), q) for q, p in zip(qs, ps)]
    ts = [p + pq[CHUNK:] for p, pq in zip(ps, pqs)]
    ts = [t + _bdot(t, pq[:CHUNK]) for t, pq in zip(ts, pqs)]
    for off in offs:
        xs = [_bdot(jnp.where(off, lm, 0.0), t) for lm, t in zip(lms, ts)]
        ts = [t - _bdot(t, x) for t, x in zip(ts, xs)]
    return ts


def _gdn_scan_kernel(qf_ref, kf_ref, vf_ref, gf_ref, gtf_ref,
                     qb_ref, kb_ref, vb_ref, gb_ref, gtb_ref,
                     of_ref, ob_ref, s_ref):
    @pl.when(pl.program_id(1) == 0)
    def _():
        s_ref[...] = jnp.zeros_like(s_ref)

    r = lax.broadcasted_iota(jnp.int32, (CHUNK, CHUNK), 0)
    c = lax.broadcasted_iota(jnp.int32, (CHUNK, CHUNK), 1)
    eye = (r == c).astype(F32)
    tri_masks = _tri_masks(r, c)
    r2 = lax.broadcasted_iota(jnp.int32, (GDN_TB, GDN_TB), 0)
    c2 = lax.broadcasted_iota(jnp.int32, (GDN_TB, GDN_TB), 1)
    same = (r2 >> 6) == (c2 >> 6)
    lower = jnp.where(same & (r2 >= c2), 1.0, 0.0).astype(BF16)
    upper = jnp.where(same & (r2 <= c2), 1.0, 0.0).astype(BF16)
    nt = (((1,), (1,)), ((), ()))
    tn = (((0,), (0,)), ((), ()))

    dirs = ((qf_ref, kf_ref, vf_ref, gf_ref, gtf_ref, of_ref),
            (qb_ref, kb_ref, vb_ref, gb_ref, gtb_ref, ob_ref))
    probs = []
    for d, (q_ref, k_ref, v_ref, g_ref, gt_ref, o_ref) in enumerate(dirs):
        fwd = d == 0
        g_all = g_ref[...]
        gc_cols = _split_dot(lower if fwd else upper, g_all)
        gc_rows = _split_dot(upper if fwd else lower, gt_ref[...], x_is_lhs=True)
        last = CHUNK - 1 if fwd else 0
        for hl in range(GDN_HG):
            l0 = hl * HEAD_DIM
            gl = 4 * hl + d
            for ci in range(GDN_NC):
                rows = slice(ci * CHUNK, (ci + 1) * CHUNK)
                gcol = gc_cols[rows, gl:gl + 1]
                probs.append(dict(
                    d=d, hl=hl, ci=ci, rows=rows, l0=l0, o_ref=o_ref,
                    incl=(r >= c) if fwd else (r <= c),
                    strict=(r > c) if fwd else (r < c),
                    q=q_ref[rows, l0:l0 + HEAD_DIM],
                    k=k_ref[rows, l0:l0 + HEAD_DIM],
                    v=v_ref[rows, l0:l0 + HEAD_DIM],
                    gcol=gcol,
                    glast=gcol[last:last + 1, :],
                    beta=g_all[rows, gl + 2:gl + 3],
                    grow=gc_rows[gl:gl + 1, rows]))

    for p in probs:
        p['decay'] = jnp.where(p['incl'], jnp.exp(jnp.where(p['incl'], p['gcol'] - p['grow'], 0.0)), 0.0)
        kq = jnp.concatenate([p['k'], p['q']], axis=0)
        p['kkqk'] = lax.dot_general(kq, p['k'], nt, preferred_element_type=F32)
    for p in probs:
        p['lm'] = jnp.where(p['strict'], p['beta'] * p['kkqk'][:CHUNK] * p['decay'], 0.0)
        p['attn'] = jnp.where(p['incl'], p['kkqk'][CHUNK:] * p['decay'], 0.0)
        eg = jnp.exp(p['gcol'])
        q, k, v = (p[n].astype(F32) for n in 'qkv')
        p['qd'] = q * eg
        p['rhs'] = jnp.concatenate([v * p['beta'], k * (p['beta'] * eg)], axis=1)
        p['kt'] = k * jnp.exp(p['glast'] - p['gcol'])
    tinvs = _tri_inverse_all([p['lm'] for p in probs], eye, tri_masks)
    for p, tinv in zip(probs, tinvs):
        p['uw'] = _bdot(tinv, p['rhs']).astype(BF16)
    for p in probs:
        au_aw = jnp.dot(p['attn'].astype(BF16), p['uw'], preferred_element_type=F32)
        ku_kw = lax.dot_general(p['kt'].astype(BF16), p['uw'], tn, preferred_element_type=F32)
        p['o0'] = au_aw[:, :HEAD_DIM]
        p['n'] = ku_kw[:, :HEAD_DIM]
        p['kq'] = jnp.concatenate([ku_kw[:, HEAD_DIM:], p['qd'] - au_aw[:, HEAD_DIM:]], axis=0).astype(BF16)

    states = [s_ref[i] for i in range(2 * GDN_HG)]
    for step in range(GDN_NC):
        for p in probs:
            if p['ci'] != (step if p['d'] == 0 else GDN_NC - 1 - step):
                continue
            si = p['d'] * GDN_HG + p['hl']
            ks_qs = jnp.dot(p['kq'], states[si].astype(BF16), preferred_element_type=F32)
            p['o_ref'][p['rows'], p['l0']:p['l0'] + HEAD_DIM] = (ks_qs[HEAD_DIM:] + p['o0']).astype(BF16)
            states[si] = states[si] * jnp.exp(p['glast']) - ks_qs[:HEAD_DIM] + p['n']
    for i in range(2 * GDN_HG):
        s_ref[i] = states[i]


def _gdn_scan(qkvg, gates, gates_t):
    ngroups = GDN_GROUPS
    hw = GDN_HG * HEAD_DIM

    def blk(p, t, bwd):
        b = p // ngroups
        lat = b * GDN_LAT_BLKS + (GDN_LAT_BLKS - t if bwd else t - 1)
        return jnp.where(t == 0, GDN_CTX_BLK + b, lat)

    def specs(bwd):
        col = lambda base: (lambda p, t: (blk(p, t, bwd), base * ngroups + p % ngroups))
        return [pl.BlockSpec((GDN_TB, hw), col(0)),
                pl.BlockSpec((GDN_TB, hw), col(1)),
                pl.BlockSpec((GDN_TB, hw), col(2)),
                pl.BlockSpec((GDN_TB, HEAD_DIM), lambda p, t: (blk(p, t, bwd), p % ngroups)),
                pl.BlockSpec((GDN_GL, GDN_TB), lambda p, t: (p % ngroups, blk(p, t, bwd)))]

    out_spec = lambda bwd: pl.BlockSpec((GDN_TB, hw), lambda p, t: (blk(p, t, bwd), p % ngroups))
    return pl.pallas_call(
        _gdn_scan_kernel,
        grid=(BATCH * ngroups, GDN_STEPS),
        in_specs=specs(False) + specs(True),
        out_specs=[out_spec(False), out_spec(True)],
        out_shape=[jax.ShapeDtypeStruct((ROWS, D), BF16)] * 2,
        scratch_shapes=[pltpu.VMEM((2 * GDN_HG, HEAD_DIM, HEAD_DIM), F32)],
        compiler_params=_params(("parallel", "arbitrary")),
        name="gdn_scan",
    )(qkvg, qkvg, qkvg, gates, gates_t, qkvg, qkvg, qkvg, gates, gates_t)


def _gdn_out_kernel(x_ref, of_ref, ob_ref, zg_ref, ng_ref, gate_ref, w_ref, o_ref):
    zg = zg_ref[...].astype(F32)
    o = of_ref[...].astype(F32) + ob_ref[...].astype(F32)
    y = _head_rms(o, ng_ref[...]) * (zg * jax.nn.sigmoid(zg))
    o_ref[...] = x_ref[...] + gate_ref[...] * jnp.dot(
        y.astype(BF16), w_ref[...], preferred_element_type=F32)


def _gdn_out(x, o_f, o_b, z, norm_g, mods, w_out):
    row = pl.BlockSpec((TM, D), lambda i: (i, 0))
    return pl.pallas_call(
        _gdn_out_kernel,
        grid=(N_TILES,),
        in_specs=[row, row, row,
                  pl.BlockSpec((TM, D), lambda i: (i, GDN_GATE_COL)),
                  pl.BlockSpec((1, HEAD_DIM), lambda i: (0, 0)),
                  _mod_spec(2),
                  pl.BlockSpec((D, D), lambda i: (0, 0))],
        out_specs=row,
        out_shape=jax.ShapeDtypeStruct((ROWS, D), F32),
        compiler_params=_params(("parallel",)),
        name="gdn_out",
    )(x, o_f, o_b, z, norm_g.reshape(1, HEAD_DIM), mods, w_out)


def _gdn_layer(x, mods, g1, w_in, conv_w, a_log, dt_bias, norm_g, w_out):
    ab = w_in[:, 4 * D:].reshape(D, 2, 2, HEADS)
    ab = jnp.transpose(ab, (0, 3, 1, 2)).reshape(D, GDN_GROUPS, GDN_GL)
    wab = jnp.pad(ab, ((0, 0), (0, 0), (0, HEAD_DIM - GDN_GL))).reshape(D, GDN_GW).astype(BF16)

    def pad_head_param(p):
        p = jnp.pad(jnp.transpose(p, (1, 0)), ((0, 0), (0, 2))).reshape(GDN_GROUPS, GDN_GL)
        return jnp.pad(p, ((0, 0), (0, HEAD_DIM - GDN_GL))).reshape(1, GDN_GW).astype(F32)

    qkvg, gates, gates_t = _gdn_proj(x, g1, mods, w_in[:, :4 * D].astype(BF16), wab, conv_w,
                                     pad_head_param(a_log), pad_head_param(dt_bias))
    o_f, o_b = _gdn_scan(qkvg, gates, gates_t)
    return _gdn_out(x, o_f, o_b, qkvg, norm_g, mods, w_out.astype(BF16))


def _sconv_out_kernel(x_ref, bg_ref, cg_ref, hv_ref, cgp_ref, hvp_ref, cgn_ref, hvn_ref,
                      cw_ref, gate_ref, w_ref, o_ref):
    f32 = lambda ref: ref[...].astype(F32)
    prev = (f32(cgp_ref) * f32(hvp_ref))[BF16_HALO - HALO:]
    nxt = (f32(cgn_ref) * f32(hvn_ref))[:HALO]
    xh = jnp.concatenate([prev, f32(cg_ref) * f32(hv_ref), nxt], axis=0)
    y = f32(bg_ref) * _conv3(xh, cw_ref[...], pl.program_id(0) * TM)
    o_ref[...] = x_ref[...] + gate_ref[...] * jnp.dot(
        y.astype(BF16), w_ref[...], preferred_element_type=F32)


def _sconv_layer(x, mods, g1, w_in, conv_w, w_out):
    z = _proj_in(x, g1, mods, w_in.astype(BF16), tn=1024)
    colspec = lambda cidx: pl.BlockSpec((TM, D), lambda i: (i, cidx))
    p1, n1 = _halo_specs(TM, D, lambda i: 1, BF16_HALO)
    p2, n2 = _halo_specs(TM, D, lambda i: 2, BF16_HALO)
    return pl.pallas_call(
        _sconv_out_kernel,
        grid=(N_TILES,),
        in_specs=[colspec(0), colspec(0), colspec(1), colspec(2), p1, p2, n1, n2,
                  pl.BlockSpec((3, D), lambda i: (0, 0)),
                  _mod_spec(2),
                  pl.BlockSpec((D, D), lambda i: (0, 0))],
        out_specs=colspec(0),
        out_shape=jax.ShapeDtypeStruct((ROWS, D), F32),
        compiler_params=_params(("parallel",)),
        name="sconv_out",
    )(x, z, z, z, z, z, z, z, conv_w, mods, w_out.astype(BF16))


def _rope_tables():
    t = jnp.arange(SEQ)
    row = (t // GRID_W).astype(F32)
    colp = (t % GRID_W).astype(F32)
    nf = DIFF_DH // 4
    inv = ROPE_BASE ** (-jnp.arange(nf, dtype=F32) / nf)
    ang = jnp.concatenate([row[:, None] * inv, colp[:, None] * inv], axis=-1)
    ang = jnp.tile(jnp.repeat(ang, 2, axis=-1), (1, HEAD_DIM // DIFF_DH))
    even = (jnp.arange(HEAD_DIM) % 2) == 0
    cos, sin = jnp.cos(ang), jnp.sin(ang)
    return cos, jnp.where(even, -sin, 0.0), jnp.where(even, 0.0, sin)


Q_SCALE = DIFF_DH ** -0.5 * math.log2(math.e)


def _diff_prep_kernel(z_ref, cos_ref, sa_ref, sb_ref, qk_ref, vt_ref):
    i, j = pl.program_id(0), pl.program_id(1)
    scale = jnp.where(j == 0, Q_SCALE, 1.0).astype(F32)

    @pl.when(jnp.logical_and(j < 2, i < N_LAT_TILES))
    def _():
        cos, sa, sb = cos_ref[...], sa_ref[...], sb_ref[...]
        for h in range(HEADS):
            sl = slice(h * HEAD_DIM, (h + 1) * HEAD_DIM)
            x = z_ref[:, sl].astype(F32)
            y = x * cos + pltpu.roll(x, HEAD_DIM - 1, 1) * sa + pltpu.roll(x, 1, 1) * sb
            qk_ref[:, sl] = (y * scale).astype(BF16)

    @pl.when(jnp.logical_and(j < 2, i >= N_LAT_TILES))
    def _():
        qk_ref[...] = (z_ref[...].astype(F32) * scale).astype(BF16)

    @pl.when(j == 2)
    def _():
        vt_ref[...] = z_ref[...].astype(F32).T.astype(BF16)


def _diff_prep(z):
    tab = pl.BlockSpec((TM, HEAD_DIM), lambda i, j: (jnp.minimum(i, N_LAT_TILES - 1) % TILES_PER_BATCH, 0))
    return pl.pallas_call(
        _diff_prep_kernel,
        grid=(N_TILES, 3),
        in_specs=[pl.BlockSpec((TM, D), lambda i, j: (i, j)), tab, tab, tab],
        out_specs=[pl.BlockSpec((TM, D), lambda i, j: (i, jnp.minimum(j, 1))),
                   pl.BlockSpec((D, TM), lambda i, j: (0, i))],
        out_shape=[jax.ShapeDtypeStruct((ROWS, 2 * D), BF16),
                   jax.ShapeDtypeStruct((D, ROWS), BF16)],
        compiler_params=_params(("parallel", "arbitrary")),
        name="diff_prep",
    )(z, *_rope_tables())


def _lambda_full(lam_ref, lambda_init):
    lf = lam_ref[...]
    a = jnp.sum(lf[0:1, :] * lf[1:2, :], axis=-1, keepdims=True)
    b = jnp.sum(lf[2:3, :] * lf[3:4, :], axis=-1, keepdims=True)
    return jnp.exp(a) - jnp.exp(b) + lambda_init


ATT_TQ = 1024
ATT_TK = 1024
ATT_QW = 256


def _split_maps(q):
    lane = lax.broadcasted_iota(jnp.int32, q.shape, 1)
    zero = jnp.zeros_like(q)
    return jnp.where(lane < DIFF_DH, q, zero), jnp.where(lane >= DIFF_DH, q, zero)


def _attn_update(k, vt, qz_ref, m_ref, l_ref, acc_ref):
    nt = (((1,), (1,)), ((), ()))
    chains = [(mi, slice(w * ATT_QW, (w + 1) * ATT_QW))
              for mi in range(2) for w in range(ATT_TQ // ATT_QW)]
    def scores(st, mi, sl):
        st['s'] = lax.dot_general(k, qz_ref[mi, sl, :], nt, preferred_element_type=F32)

    def stats(st, mi, sl):
        m_old = m_ref[mi, :, sl]
        st['m'] = jnp.maximum(m_old, jnp.max(st['s'], axis=0, keepdims=True))
        st['alpha'] = jnp.exp2(m_old - st['m'])
        m_ref[mi, :, sl] = st['m']

    def probs(st, mi, sl):
        p = jnp.exp2(st.pop('s') - st['m'])
        l_ref[mi, :, sl] = st['alpha'] * l_ref[mi, :, sl] + jnp.sum(p, axis=0, keepdims=True)
        st['p'] = p.astype(BF16)

    def values(st, mi, sl):
        pv = jnp.dot(vt, st.pop('p'), preferred_element_type=F32)
        acc_ref[mi, :, sl] = st['alpha'] * acc_ref[mi, :, sl] + pv

    state = [{} for _ in chains]
    for stage in (scores, stats, probs, values):
        for st, chain in zip(state, chains):
            stage(st, *chain)


def _attn_lat_kernel(lam_ref, q_ref, kc_ref, vct_ref, k_ref, vt_ref, o_ref,
                     qz_ref, m_ref, l_ref, acc_ref, *, lambda_init):
    j = pl.program_id(2)

    @pl.when(j == 0)
    def _():
        q1, q2 = _split_maps(q_ref[...])
        qz_ref[0] = q1
        qz_ref[1] = q2
        m_ref[...] = jnp.full_like(m_ref, -jnp.inf)
        l_ref[...] = jnp.zeros_like(l_ref)
        acc_ref[...] = jnp.zeros_like(acc_ref)
        _attn_update(kc_ref[...], vct_ref[...], qz_ref, m_ref, l_ref, acc_ref)

    _attn_update(k_ref[...], vt_ref[...], qz_ref, m_ref, l_ref, acc_ref)

    @pl.when(j == pl.num_programs(2) - 1)
    def _():
        lmb = _lambda_full(lam_ref, lambda_init)
        ot = acc_ref[0] / l_ref[0] - lmb * (acc_ref[1] / l_ref[1])
        o_ref[...] = ot.T


def _attn_ctx_kernel(lam_ref, q_ref, k_ref, vt_ref, o_ref, *, lambda_init):
    nt = (((1,), (1,)), ((), ()))
    outs = []
    for qz in _split_maps(q_ref[...]):
        s = lax.dot_general(k_ref[...], qz, nt, preferred_element_type=F32)
        p = jnp.exp2(s - jnp.max(s, axis=0, keepdims=True))
        o = jnp.dot(vt_ref[...], p.astype(BF16), preferred_element_type=F32)
        outs.append(o / jnp.sum(p, axis=0, keepdims=True))
    o_ref[...] = (outs[0] - _lambda_full(lam_ref, lambda_init) * outs[1]).T


def _diff_attention(qk, vt, lam, lambda_init):
    nq, nk = SEQ // ATT_TQ, SEQ // ATT_TK
    ctx_blk = LAT_ROWS // CTX
    lam_spec3 = pl.BlockSpec((4, DIFF_DH), lambda p, i, j: (0, 0))
    o_lat = pl.pallas_call(
        functools.partial(_attn_lat_kernel, lambda_init=lambda_init),
        grid=(BATCH * HEADS, nq, nk),
        in_specs=[lam_spec3,
                  pl.BlockSpec((ATT_TQ, HEAD_DIM), lambda p, i, j: ((p // HEADS) * nq + i, p % HEADS)),
                  pl.BlockSpec((CTX, HEAD_DIM), lambda p, i, j: (ctx_blk + p // HEADS, HEADS + p % HEADS)),
                  pl.BlockSpec((HEAD_DIM, CTX), lambda p, i, j: (p % HEADS, ctx_blk + p // HEADS)),
                  pl.BlockSpec((ATT_TK, HEAD_DIM), lambda p, i, j: ((p // HEADS) * nk + j, HEADS + p % HEADS)),
                  pl.BlockSpec((HEAD_DIM, ATT_TK), lambda p, i, j: (p % HEADS, (p // HEADS) * nk + j))],
        out_specs=pl.BlockSpec((ATT_TQ, HEAD_DIM), lambda p, i, j: ((p // HEADS) * nq + i, p % HEADS)),
        out_shape=jax.ShapeDtypeStruct((LAT_ROWS, D), F32),
        scratch_shapes=[pltpu.VMEM((2, ATT_TQ, HEAD_DIM), BF16),
                        pltpu.VMEM((2, 1, ATT_TQ), F32), pltpu.VMEM((2, 1, ATT_TQ), F32),
                        pltpu.VMEM((2, HEAD_DIM, ATT_TQ), F32)],
        compiler_params=_params(("parallel", "parallel", "arbitrary")),
        name="diff_attn_lat",
    )(lam, qk, qk, vt, qk, vt)
    o_ctx = pl.pallas_call(
        functools.partial(_attn_ctx_kernel, lambda_init=lambda_init),
        grid=(BATCH, HEADS),
        in_specs=[pl.BlockSpec((4, DIFF_DH), lambda b, h: (0, 0)),
                  pl.BlockSpec((CTX, HEAD_DIM), lambda b, h: (ctx_blk + b, h)),
                  pl.BlockSpec((CTX, HEAD_DIM), lambda b, h: (ctx_blk + b, HEADS + h)),
                  pl.BlockSpec((HEAD_DIM, CTX), lambda b, h: (h, ctx_blk + b))],
        out_specs=pl.BlockSpec((CTX, HEAD_DIM), lambda b, h: (b, h)),
        out_shape=jax.ShapeDtypeStruct((CTX_ROWS, D), F32),
        compiler_params=_params(("parallel", "parallel")),
        name="diff_attn_ctx",
    )(lam, qk, qk, vt)
    return o_lat, o_ctx


def _diff_out_kernel(x_ref, ol_ref, oc_ref, ng_ref, gate_ref, w_ref, o_ref, *, out_scale):
    is_ctx = pl.program_id(0) >= N_LAT_TILES
    o = jnp.where(is_ctx, oc_ref[...], ol_ref[...])
    y = _head_rms(o, ng_ref[...]) * out_scale
    o_ref[...] = x_ref[...] + gate_ref[...] * jnp.dot(
        y.astype(BF16), w_ref[...], preferred_element_type=F32)


def _diff_layer(x, mods, g1, w_in, lam, norm_g, w_out, lambda_init):
    z = _proj_in(x, g1, mods, w_in.astype(BF16), tn=1024)
    o_lat, o_ctx = _diff_attention(*_diff_prep(z), lam, lambda_init)
    row = pl.BlockSpec((TM, D), lambda i: (i, 0))
    return pl.pallas_call(
        functools.partial(_diff_out_kernel, out_scale=1.0 - lambda_init),
        grid=(N_TILES,),
        in_specs=[row,
                  pl.BlockSpec((TM, D), lambda i: (jnp.minimum(i, N_LAT_TILES - 1), 0)),
                  pl.BlockSpec((TM, D), lambda i: (0, 0)),
                  pl.BlockSpec((1, HEAD_DIM), lambda i: (0, 0)),
                  _mod_spec(2),
                  pl.BlockSpec((D, D), lambda i: (0, 0))],
        out_specs=row,
        out_shape=jax.ShapeDtypeStruct((ROWS, D), F32),
        compiler_params=_params(("parallel",)),
        name="diff_out",
    )(x, o_lat, o_ctx, norm_g.reshape(1, HEAD_DIM), mods, w_out.astype(BF16))


def kernel(x, c, ctx, c_ctx, norm1_g, norm2_g, ada_w, ada_b, mlp_w1, mlp_w2, gdn_w_in, gdn_conv, gdn_a_log, gdn_dt_bias, gdn_norm_g, gdn_w_out, sconv_w_in, sconv_conv, sconv_w_out, diff_w_in, diff_lambda, diff_norm_g, diff_w_out, final_g):
    mods = _ada(c, c_ctx, ada_w, ada_b)
    h = jnp.concatenate([x.reshape(LAT_ROWS, D), ctx.reshape(CTX_ROWS, D)], axis=0)
    for i in range(DEPTH):
        kind, j = i % 3, i // 3
        if kind == 0:
            h = _gdn_layer(h, mods[i], norm1_g[i], gdn_w_in[j], gdn_conv[j], gdn_a_log[j],
                           gdn_dt_bias[j], gdn_norm_g[j], gdn_w_out[j])
        elif kind == 1:
            h = _sconv_layer(h, mods[i], norm1_g[i], sconv_w_in[j], sconv_conv[j], sconv_w_out[j])
        else:
            lambda_init = 0.8 - 0.6 * math.exp(-0.3 * i)
            h = _diff_layer(h, mods[i], norm1_g[i], diff_w_in[j], diff_lambda[j],
                            diff_norm_g[j], diff_w_out[j], lambda_init)
        h = _mlp(h, norm2_g[i], mods[i], mlp_w1[i].astype(BF16), mlp_w2[i].astype(BF16),
                 final_g, last=i == DEPTH - 1)
    return h.reshape(BATCH, SEQ, D)
```

```python
import functools
import math

import jax
import jax.numpy as jnp
from jax import lax
from jax.experimental import pallas as pl
from jax.experimental.pallas import tpu as pltpu

F32 = jnp.float32
BF16 = jnp.bfloat16

D = 1024
BATCH = 2
SEQ = 8192
CTX = 256
DEPTH = 4
N_ADA = 6
D_FF = 4 * D
EPS = 1e-6
HEADS = 8
HEAD_DIM = 128
CHUNK = 64
DIFF_DH = 64
GRID_W = 64
ROPE_BASE = 10000.0

LAT_ROWS = BATCH * SEQ
CTX_ROWS = BATCH * CTX
ROWS = LAT_ROWS + CTX_ROWS
TM = 512
N_TILES = ROWS // TM
N_LAT_TILES = LAT_ROWS // TM
TILES_PER_BATCH = SEQ // TM
VMEM_LIMIT = 56 * 1024 * 1024


def _params(sem, vmem=VMEM_LIMIT):
    return pltpu.CompilerParams(dimension_semantics=sem, vmem_limit_bytes=vmem)


def _mod_row(i):
    return jnp.where(i >= N_LAT_TILES, 0, 1 + i // TILES_PER_BATCH)


def _mod_spec(k):
    return pl.BlockSpec((None, None, 1, D), lambda i, *_: (_mod_row(i), k, 0, 0))


def _norm_mod(x, g, sh, sc):
    ms = jnp.mean(x * x, axis=-1, keepdims=True)
    return (x * lax.rsqrt(ms + EPS) * g) * (1.0 + sc) + sh


def _head_rms(o, g):
    outs = []
    for h in range(HEADS):
        oh = o[:, h * HEAD_DIM:(h + 1) * HEAD_DIM]
        ms = jnp.mean(oh * oh, axis=-1, keepdims=True)
        outs.append(oh * lax.rsqrt(ms + EPS) * g)
    return jnp.concatenate(outs, axis=1)


HALO = 8
BF16_HALO = 16


def _is_seq_start(r):
    starts = (0, SEQ, LAT_ROWS, LAT_ROWS + CTX)
    return functools.reduce(jnp.logical_or, [r == s for s in starts])


def _zero_row(a, row, cond):
    s0 = row // 8 * 8
    sub = lax.broadcasted_iota(jnp.int32, (8, 1), 0)
    slab = jnp.where(jnp.logical_and(sub == row - s0, cond), 0.0, a[s0:s0 + 8])
    parts = [a[:s0]] * (s0 > 0) + [slab] + [a[s0 + 8:]] * (s0 + 8 < a.shape[0])
    return jnp.concatenate(parts, axis=0)


def _conv3(xh, w, r0):
    n = xh.shape[0]
    tm = n - 2 * HALO
    x_dn = pltpu.roll(xh, 1, 0)[HALO:HALO + tm]
    x_up = pltpu.roll(xh, n - 1, 0)[HALO:HALO + tm]
    for c in range(0, tm, CTX):
        nxt = r0 + c + CTX
        x_dn = _zero_row(x_dn, c, _is_seq_start(r0 + c))
        x_up = _zero_row(x_up, c + CTX - 1, jnp.logical_or(_is_seq_start(nxt), nxt == ROWS))
    return x_dn * w[0:1, :] + xh[HALO:HALO + tm] * w[1:2, :] + x_up * w[2:3, :]


def _halo_specs(tm, tc, col_of, rows):
    nb = ROWS // rows
    prev = pl.BlockSpec((rows, tc), lambda i, *a: (jnp.maximum(i * (tm // rows) - 1, 0), col_of(i, *a)))
    nxt = pl.BlockSpec((rows, tc), lambda i, *a: (jnp.minimum((i + 1) * (tm // rows), nb - 1), col_of(i, *a)))
    return prev, nxt


def _ada_kernel(s_ref, w_ref, b_ref, o_ref):
    s = s_ref[...]
    s = s * jax.nn.sigmoid(s)
    o_ref[...] = jnp.dot(s.astype(BF16), w_ref[...].astype(BF16),
                         preferred_element_type=F32) + b_ref[...]


def _ada(c, c_ctx, ada_w, ada_b):
    tn = 1536
    s = jnp.concatenate([c_ctx[None, :], c, jnp.zeros((5, D), F32)], axis=0)
    out = pl.pallas_call(
        _ada_kernel,
        grid=(DEPTH, N_ADA * D // tn),
        in_specs=[pl.BlockSpec((8, D), lambda l, j: (0, 0)),
                  pl.BlockSpec((None, D, tn), lambda l, j: (l, 0, j)),
                  pl.BlockSpec((None, 1, tn), lambda l, j: (l, 0, j))],
        out_specs=pl.BlockSpec((None, 8, tn), lambda l, j: (l, 0, j)),
        out_shape=jax.ShapeDtypeStruct((DEPTH, 8, N_ADA * D), F32),
        compiler_params=_params(("parallel", "parallel")),
        name="ada",
    )(s, ada_w, ada_b.reshape(DEPTH, 1, N_ADA * D))
    return out[:, :3].reshape(DEPTH, 3, N_ADA, 1, D)


def _proj_in_kernel(x_ref, g_ref, sh_ref, sc_ref, w_ref, o_ref, h_ref):
    @pl.when(pl.program_id(1) == 0)
    def _():
        h_ref[...] = _norm_mod(x_ref[...], g_ref[...], sh_ref[...], sc_ref[...]).astype(BF16)

    o_ref[...] = jnp.dot(h_ref[...], w_ref[...], preferred_element_type=F32).astype(BF16)


def _proj_in(x, g, mods, w, tn):
    n = w.shape[1]
    return pl.pallas_call(
        _proj_in_kernel,
        grid=(N_TILES, n // tn),
        in_specs=[pl.BlockSpec((TM, D), lambda i, j: (i, 0)),
                  pl.BlockSpec((1, D), lambda i, j: (0, 0)),
                  _mod_spec(0), _mod_spec(1),
                  pl.BlockSpec((D, tn), lambda i, j: (0, j))],
        out_specs=pl.BlockSpec((TM, tn), lambda i, j: (i, j)),
        out_shape=jax.ShapeDtypeStruct((ROWS, n), BF16),
        scratch_shapes=[pltpu.VMEM((TM, D), BF16)],
        compiler_params=_params(("parallel", "arbitrary")),
        name="proj_in",
    )(x, g.reshape(1, D), mods, mods, w)


def _mlp_kernel(x_ref, g_ref, sh_ref, sc_ref, gate_ref, w1_ref, w2_ref, fg_ref, o_ref,
                h_ref, acc_ref, *, final_norm):
    j = pl.program_id(1)

    @pl.when(j == 0)
    def _():
        h_ref[...] = _norm_mod(x_ref[...], g_ref[...], sh_ref[...], sc_ref[...]).astype(BF16)
        acc_ref[...] = jnp.zeros_like(acc_ref)

    a = jnp.dot(h_ref[...], w1_ref[...], preferred_element_type=F32)
    a = jnp.square(jnp.maximum(a, 0.0))
    acc_ref[...] += jnp.dot(a.astype(BF16), w2_ref[...], preferred_element_type=F32)

    @pl.when(j == pl.num_programs(1) - 1)
    def _():
        y = x_ref[...] + gate_ref[...] * acc_ref[...]
        if final_norm:
            ms = jnp.mean(y * y, axis=-1, keepdims=True)
            y = y * lax.rsqrt(ms + EPS) * fg_ref[...]
        o_ref[...] = y


def _mlp(x, g, mods, w1, w2, final_g, *, last):
    tf = 2048
    n_tiles = N_LAT_TILES if last else N_TILES
    return pl.pallas_call(
        functools.partial(_mlp_kernel, final_norm=last),
        grid=(n_tiles, D_FF // tf),
        in_specs=[pl.BlockSpec((TM, D), lambda i, j: (i, 0)),
                  pl.BlockSpec((1, D), lambda i, j: (0, 0)),
                  _mod_spec(3), _mod_spec(4), _mod_spec(5),
                  pl.BlockSpec((D, tf), lambda i, j: (0, j)),
                  pl.BlockSpec((tf, D), lambda i, j: (j, 0)),
                  pl.BlockSpec((1, D), lambda i, j: (0, 0))],
        out_specs=pl.BlockSpec((TM, D), lambda i, j: (i, 0)),
        out_shape=jax.ShapeDtypeStruct((n_tiles * TM, D), F32),
        scratch_shapes=[pltpu.VMEM((TM, D), BF16), pltpu.VMEM((TM, D), F32)],
        compiler_params=_params(("parallel", "arbitrary")),
        name="mlp",
    )(x, g.reshape(1, D), mods, mods, mods, w1, w2, final_g.reshape(1, D))


GDN_HG = 8
GDN_GROUPS = HEADS // GDN_HG
GDN_GATE_COL = 3
GDN_GL = 4 * GDN_HG
GDN_GW = GDN_GROUPS * HEAD_DIM
GDN_SLAB = 256


def _gdn_proj_kernel(x_ref, xp_ref, xn_ref, g_ref, sh_ref, sc_ref, w_ref, wab_ref, cw_ref,
                     alog_ref, dtb_ref, o_ref, gates_ref, gates_t_ref, h_ref):
    i, j = pl.program_id(0), pl.program_id(1)
    rows = slice(HALO, HALO + TM)

    @pl.when(j == 0)
    def _():
        x = jnp.concatenate([xp_ref[...], x_ref[...], xn_ref[...]], axis=0)
        h_ref[...] = _norm_mod(x, g_ref[...], sh_ref[...], sc_ref[...]).astype(BF16)
        ab = jnp.dot(h_ref[...], wab_ref[...], preferred_element_type=F32)[rows]
        lane = lax.broadcasted_iota(jnp.int32, ab.shape, 1) & 3
        t = ab + dtb_ref[...]
        softplus = jnp.maximum(t, 0.0) + jnp.log1p(jnp.exp(-jnp.abs(t)))
        gates = jnp.where(lane < 2, -jnp.exp(alog_ref[...]) * softplus, jax.nn.sigmoid(ab))
        gates_ref[...] = gates
        for hg in range(GDN_GROUPS):
            gt = gates[:, hg * HEAD_DIM:(hg + 1) * HEAD_DIM].T
            gates_t_ref[hg * GDN_GL:(hg + 1) * GDN_GL, :] = gt[0:GDN_GL, :]

    def columns(mode):
        scale = jnp.where(j == 0, HEAD_DIM ** -0.5, 1.0).astype(F32)
        for c0 in range(0, D, GDN_SLAB):
            cs = slice(c0, c0 + GDN_SLAB)
            z = jnp.dot(h_ref[...], w_ref[:, cs], preferred_element_type=F32)
            if mode == 'gate':
                o_ref[:, cs] = z[rows].astype(BF16)
                continue
            y = _conv3(z, cw_ref[:, cs], i * TM)
            y = y * jax.nn.sigmoid(y)
            if mode == 'v':
                o_ref[:, cs] = y.astype(BF16)
                continue
            for h0 in range(0, GDN_SLAB, HEAD_DIM):
                yh = y[:, h0:h0 + HEAD_DIM]
                ss = jnp.sum(yh * yh, axis=-1, keepdims=True)
                o_ref[:, c0 + h0:c0 + h0 + HEAD_DIM] = (yh * (lax.rsqrt(ss + EPS) * scale)).astype(BF16)

    pl.when(j < 2)(lambda: columns('qk'))
    pl.when(j == 2)(lambda: columns('v'))
    pl.when(j == 3)(lambda: columns('gate'))


def _gdn_proj(x, g, mods, w, wab, conv_w, alog_pad, dtb_pad):
    prev, nxt = _halo_specs(TM, D, lambda i, j: 0, HALO)
    full = lambda shape: pl.BlockSpec(shape, lambda i, j: (0, 0))
    return pl.pallas_call(
        _gdn_proj_kernel,
        grid=(N_TILES, 4),
        in_specs=[pl.BlockSpec((TM, D), lambda i, j: (i, 0)), prev, nxt,
                  full((1, D)), _mod_spec(0), _mod_spec(1),
                  pl.BlockSpec((D, D), lambda i, j: (0, j)),
                  full((D, GDN_GW)),
                  pl.BlockSpec((3, D), lambda i, j: (0, jnp.minimum(j, 2))),
                  full((1, GDN_GW)), full((1, GDN_GW))],
        out_specs=[pl.BlockSpec((TM, D), lambda i, j: (i, j)),
                   pl.BlockSpec((TM, GDN_GW), lambda i, j: (i, 0)),
                   pl.BlockSpec((GDN_GROUPS * GDN_GL, TM), lambda i, j: (0, i))],
        out_shape=[jax.ShapeDtypeStruct((ROWS, 4 * D), BF16),
                   jax.ShapeDtypeStruct((ROWS, GDN_GW), F32),
                   jax.ShapeDtypeStruct((GDN_GROUPS * GDN_GL, ROWS), F32)],
        scratch_shapes=[pltpu.VMEM((TM + 2 * HALO, D), BF16)],
        compiler_params=_params(("parallel", "arbitrary")),
        name="gdn_proj",
    )(x, x, x, g.reshape(1, D), mods, mods, w, wab, conv_w, alog_pad, dtb_pad)


GDN_TB = 256
GDN_NC = GDN_TB // CHUNK
GDN_STEPS = (CTX + SEQ) // GDN_TB
GDN_CTX_BLK = LAT_ROWS // GDN_TB
GDN_LAT_BLKS = SEQ // GDN_TB


def _split_dot(a_exact, x, *, x_is_lhs=False):
    hi = x.astype(BF16)
    r1 = x - hi.astype(F32)
    mid = r1.astype(BF16)
    lo = (r1 - mid.astype(F32)).astype(BF16)
    out = None
    for part in (hi, mid, lo):
        t = (jnp.dot(part, a_exact, preferred_element_type=F32) if x_is_lhs
             else jnp.dot(a_exact, part, preferred_element_type=F32))
        out = t if out is None else out + t
    return out


TRI_BASE_SHIFT = 3


def _tri_masks(r, c):
    same = lambda shift: (r >> shift) == (c >> shift)
    offs = [same(shift + 1) & jnp.logical_not(same(shift)) for shift in range(TRI_BASE_SHIFT, 6)]
    return same(TRI_BASE_SHIFT), offs


def _bdot(a, b):
    return jnp.dot(a.astype(BF16), b.astype(BF16), preferred_element_type=F32)


def _tri_inverse_all(lms, eye, masks):
    base, offs = masks
    lds = [jnp.where(base, lm, 0.0) for lm in lms]
    ps = [eye - ld for ld in lds]
    qs = [_bdot(ld, ld) for ld in lds]
    pqs = [_bdot(jnp.concatenate([q, p], axis=0), q) for q, p in zip(qs, ps)]
    ts = [p + pq[CHUNK:] for p, pq in zip(ps, pqs)]
    ts = [t + _bdot(t, pq[:CHUNK]) for t, pq in zip(ts, pqs)]
    for off in offs:
        xs = [_bdot(jnp.where(off, lm, 0.0), t) for lm, t in zip(lms, ts)]
        ts = [t - _bdot(t, x) for t, x in zip(ts, xs)]
    return ts


def _gdn_scan_kernel(qf_ref, kf_ref, vf_ref, gf_ref, gtf_ref,
                     qb_ref, kb_ref, vb_ref, gb_ref, gtb_ref,
                     of_ref, ob_ref, s_ref):
    @pl.when(pl.program_id(1) == 0)
    def _():
        s_ref[...] = jnp.zeros_like(s_ref)

    r = lax.broadcasted_iota(jnp.int32, (CHUNK, CHUNK), 0)
    c = lax.broadcasted_iota(jnp.int32, (CHUNK, CHUNK), 1)
    eye = (r == c).astype(F32)
    tri_masks = _tri_masks(r, c)
    r2 = lax.broadcasted_iota(jnp.int32, (GDN_TB, GDN_TB), 0)
    c2 = lax.broadcasted_iota(jnp.int32, (GDN_TB, GDN_TB), 1)
    same = (r2 >> 6) == (c2 >> 6)
    lower = jnp.where(same & (r2 >= c2), 1.0, 0.0).astype(BF16)
    upper = jnp.where(same & (r2 <= c2), 1.0, 0.0).astype(BF16)
    nt = (((1,), (1,)), ((), ()))
    tn = (((0,), (0,)), ((), ()))

    dirs = ((qf_ref, kf_ref, vf_ref, gf_ref, gtf_ref, of_ref),
            (qb_ref, kb_ref, vb_ref, gb_ref, gtb_ref, ob_ref))
    probs = []
    for d, (q_ref, k_ref, v_ref, g_ref, gt_ref, o_ref) in enumerate(dirs):
        fwd = d == 0
        g_all = g_ref[...]
        gc_cols = _split_dot(lower if fwd else upper, g_all)
        gc_rows = _split_dot(upper if fwd else lower, gt_ref[...], x_is_lhs=True)
        last = CHUNK - 1 if fwd else 0
        for hl in range(GDN_HG):
            l0 = hl * HEAD_DIM
            gl = 4 * hl + d
            for ci in range(GDN_NC):
                rows = slice(ci * CHUNK, (ci + 1) * CHUNK)
                gcol = gc_cols[rows, gl:gl + 1]
                probs.append(dict(
                    d=d, hl=hl, ci=ci, rows=rows, l0=l0, o_ref=o_ref,
                    incl=(r >= c) if fwd else (r <= c),
                    strict=(r > c) if fwd else (r < c),
                    q=q_ref[rows, l0:l0 + HEAD_DIM],
                    k=k_ref[rows, l0:l0 + HEAD_DIM],
                    v=v_ref[rows, l0:l0 + HEAD_DIM],
                    gcol=gcol,
                    glast=gcol[last:last + 1, :],
                    beta=g_all[rows, gl + 2:gl + 3],
                    grow=gc_rows[gl:gl + 1, rows]))

    for p in probs:
        p['decay'] = jnp.where(p['incl'], jnp.exp(jnp.where(p['incl'], p['gcol'] - p['grow'], 0.0)), 0.0)
        kq = jnp.concatenate([p['k'], p['q']], axis=0)
        p['kkqk'] = lax.dot_general(kq, p['k'], nt, preferred_element_type=F32)
    for p in probs:
        p['lm'] = jnp.where(p['strict'], p['beta'] * p['kkqk'][:CHUNK] * p['decay'], 0.0)
        p['attn'] = jnp.where(p['incl'], p['kkqk'][CHUNK:] * p['decay'], 0.0)
        eg = jnp.exp(p['gcol'])
        q, k, v = (p[n].astype(F32) for n in 'qkv')
        p['qd'] = q * eg
        p['rhs'] = jnp.concatenate([v * p['beta'], k * (p['beta'] * eg)], axis=1)
        p['kt'] = k * jnp.exp(p['glast'] - p['gcol'])
    tinvs = _tri_inverse_all([p['lm'] for p in probs], eye, tri_masks)
    for p, tinv in zip(probs, tinvs):
        p['uw'] = _bdot(tinv, p['rhs']).astype(BF16)
    for p in probs:
        au_aw = jnp.dot(p['attn'].astype(BF16), p['uw'], preferred_element_type=F32)
        ku_kw = lax.dot_general(p['kt'].astype(BF16), p['uw'], tn, preferred_element_type=F32)
        p['o0'] = au_aw[:, :HEAD_DIM]
        p['n'] = ku_kw[:, :HEAD_DIM]
        p['kq'] = jnp.concatenate([ku_kw[:, HEAD_DIM:], p['qd'] - au_aw[:, HEAD_DIM:]], axis=0).astype(BF16)

    states = [s_ref[i] for i in range(2 * GDN_HG)]
    for step in range(GDN_NC):
        for p in probs:
            if p['ci'] != (step if p['d'] == 0 else GDN_NC - 1 - step):
                continue
            si = p['d'] * GDN_HG + p['hl']
            ks_qs = jnp.dot(p['kq'], states[si].astype(BF16), preferred_element_type=F32)
            p['o_ref'][p['rows'], p['l0']:p['l0'] + HEAD_DIM] = (ks_qs[HEAD_DIM:] + p['o0']).astype(BF16)
            states[si] = states[si] * jnp.exp(p['glast']) - ks_qs[:HEAD_DIM] + p['n']
    for i in range(2 * GDN_HG):
        s_ref[i] = states[i]


def _gdn_scan(qkvg, gates, gates_t):
    ngroups = GDN_GROUPS
    hw = GDN_HG * HEAD_DIM

    def blk(p, t, bwd):
        b = p // ngroups
        lat = b * GDN_LAT_BLKS + (GDN_LAT_BLKS - t if bwd else t - 1)
        return jnp.where(t == 0, GDN_CTX_BLK + b, lat)

    def specs(bwd):
        col = lambda base: (lambda p, t: (blk(p, t, bwd), base * ngroups + p % ngroups))
        return [pl.BlockSpec((GDN_TB, hw), col(0)),
                pl.BlockSpec((GDN_TB, hw), col(1)),
                pl.BlockSpec((GDN_TB, hw), col(2)),
                pl.BlockSpec((GDN_TB, HEAD_DIM), lambda p, t: (blk(p, t, bwd), p % ngroups)),
                pl.BlockSpec((GDN_GL, GDN_TB), lambda p, t: (p % ngroups, blk(p, t, bwd)))]

    out_spec = lambda bwd: pl.BlockSpec((GDN_TB, hw), lambda p, t: (blk(p, t, bwd), p % ngroups))
    return pl.pallas_call(
        _gdn_scan_kernel,
        grid=(BATCH * ngroups, GDN_STEPS),
        in_specs=specs(False) + specs(True),
        out_specs=[out_spec(False), out_spec(True)],
        out_shape=[jax.ShapeDtypeStruct((ROWS, D), BF16)] * 2,
        scratch_shapes=[pltpu.VMEM((2 * GDN_HG, HEAD_DIM, HEAD_DIM), F32)],
        compiler_params=_params(("parallel", "arbitrary")),
        name="gdn_scan",
    )(qkvg, qkvg, qkvg, gates, gates_t, qkvg, qkvg, qkvg, gates, gates_t)


def _gdn_out_kernel(x_ref, of_ref, ob_ref, zg_ref, ng_ref, gate_ref, w_ref, o_ref):
    zg = zg_ref[...].astype(F32)
    o = of_ref[...].astype(F32) + ob_ref[...].astype(F32)
    y = _head_rms(o, ng_ref[...]) * (zg * jax.nn.sigmoid(zg))
    o_ref[...] = x_ref[...] + gate_ref[...] * jnp.dot(
        y.astype(BF16), w_ref[...], preferred_element_type=F32)


def _gdn_out(x, o_f, o_b, z, norm_g, mods, w_out):
    row = pl.BlockSpec((TM, D), lambda i: (i, 0))
    return pl.pallas_call(
        _gdn_out_kernel,
        grid=(N_TILES,),
        in_specs=[row, row, row,
                  pl.BlockSpec((TM, D), lambda i: (i, GDN_GATE_COL)),
                  pl.BlockSpec((1, HEAD_DIM), lambda i: (0, 0)),
                  _mod_spec(2),
                  pl.BlockSpec((D, D), lambda i: (0, 0))],
        out_specs=row,
        out_shape=jax.ShapeDtypeStruct((ROWS, D), F32),
        compiler_params=_params(("parallel",)),
        name="gdn_out",
    )(x, o_f, o_b, z, norm_g.reshape(1, HEAD_DIM), mods, w_out)


def _gdn_layer(x, mods, g1, w_in, conv_w, a_log, dt_bias, norm_g, w_out):
    ab = w_in[:, 4 * D:].reshape(D, 2, 2, HEADS)
    ab = jnp.transpose(ab, (0, 3, 1, 2)).reshape(D, GDN_GROUPS, GDN_GL)
    wab = jnp.pad(ab, ((0, 0), (0, 0), (0, HEAD_DIM - GDN_GL))).reshape(D, GDN_GW).astype(BF16)

    def pad_head_param(p):
        p = jnp.pad(jnp.transpose(p, (1, 0)), ((0, 0), (0, 2))).reshape(GDN_GROUPS, GDN_GL)
        return jnp.pad(p, ((0, 0), (0, HEAD_DIM - GDN_GL))).reshape(1, GDN_GW).astype(F32)

    qkvg, gates, gates_t = _gdn_proj(x, g1, mods, w_in[:, :4 * D].astype(BF16), wab, conv_w,
                                     pad_head_param(a_log), pad_head_param(dt_bias))
    o_f, o_b = _gdn_scan(qkvg, gates, gates_t)
    return _gdn_out(x, o_f, o_b, qkvg, norm_g, mods, w_out.astype(BF16))


def _sconv_out_kernel(x_ref, bg_ref, cg_ref, hv_ref, cgp_ref, hvp_ref, cgn_ref, hvn_ref,
                      cw_ref, gate_ref, w_ref, o_ref):
    f32 = lambda ref: ref[...].astype(F32)
    prev = (f32(cgp_ref) * f32(hvp_ref))[BF16_HALO - HALO:]
    nxt = (f32(cgn_ref) * f32(hvn_ref))[:HALO]
    xh = jnp.concatenate([prev, f32(cg_ref) * f32(hv_ref), nxt], axis=0)
    y = f32(bg_ref) * _conv3(xh, cw_ref[...], pl.program_id(0) * TM)
    o_ref[...] = x_ref[...] + gate_ref[...] * jnp.dot(
        y.astype(BF16), w_ref[...], preferred_element_type=F32)


def _sconv_layer(x, mods, g1, w_in, conv_w, w_out):
    z = _proj_in(x, g1, mods, w_in.astype(BF16), tn=1024)
    colspec = lambda cidx: pl.BlockSpec((TM, D), lambda i: (i, cidx))
    p1, n1 = _halo_specs(TM, D, lambda i: 1, BF16_HALO)
    p2, n2 = _halo_specs(TM, D, lambda i: 2, BF16_HALO)
    return pl.pallas_call(
        _sconv_out_kernel,
        grid=(N_TILES,),
        in_specs=[colspec(0), colspec(0), colspec(1), colspec(2), p1, p2, n1, n2,
                  pl.BlockSpec((3, D), lambda i: (0, 0)),
                  _mod_spec(2),
                  pl.BlockSpec((D, D), lambda i: (0, 0))],
        out_specs=colspec(0),
        out_shape=jax.ShapeDtypeStruct((ROWS, D), F32),
        compiler_params=_params(("parallel",)),
        name="sconv_out",
    )(x, z, z, z, z, z, z, z, conv_w, mods, w_out.astype(BF16))


def _rope_tables():
    t = jnp.arange(SEQ)
    row = (t // GRID_W).astype(F32)
    colp = (t % GRID_W).astype(F32)
    nf = DIFF_DH // 4
    inv = ROPE_BASE ** (-jnp.arange(nf, dtype=F32) / nf)
    ang = jnp.concatenate([row[:, None] * inv, colp[:, None] * inv], axis=-1)
    ang = jnp.tile(jnp.repeat(ang, 2, axis=-1), (1, HEAD_DIM // DIFF_DH))
    even = (jnp.arange(HEAD_DIM) % 2) == 0
    cos, sin = jnp.cos(ang), jnp.sin(ang)
    return cos, jnp.where(even, -sin, 0.0), jnp.where(even, 0.0, sin)


Q_SCALE = DIFF_DH ** -0.5 * math.log2(math.e)


def _diff_prep_kernel(z_ref, cos_ref, sa_ref, sb_ref, qk_ref, vt_ref):
    i, j = pl.program_id(0), pl.program_id(1)
    scale = jnp.where(j == 0, Q_SCALE, 1.0).astype(F32)

    @pl.when(jnp.logical_and(j < 2, i < N_LAT_TILES))
    def _():
        cos, sa, sb = cos_ref[...], sa_ref[...], sb_ref[...]
        for h in range(HEADS):
            sl = slice(h * HEAD_DIM, (h + 1) * HEAD_DIM)
            x = z_ref[:, sl].astype(F32)
            y = x * cos + pltpu.roll(x, HEAD_DIM - 1, 1) * sa + pltpu.roll(x, 1, 1) * sb
            qk_ref[:, sl] = (y * scale).astype(BF16)

    @pl.when(jnp.logical_and(j < 2, i >= N_LAT_TILES))
    def _():
        qk_ref[...] = (z_ref[...].astype(F32) * scale).astype(BF16)

    @pl.when(j == 2)
    def _():
        vt_ref[...] = z_ref[...].astype(F32).T.astype(BF16)


def _diff_prep(z):
    tab = pl.BlockSpec((TM, HEAD_DIM), lambda i, j: (jnp.minimum(i, N_LAT_TILES - 1) % TILES_PER_BATCH, 0))
    return pl.pallas_call(
        _diff_prep_kernel,
        grid=(N_TILES, 3),
        in_specs=[pl.BlockSpec((TM, D), lambda i, j: (i, j)), tab, tab, tab],
        out_specs=[pl.BlockSpec((TM, D), lambda i, j: (i, jnp.minimum(j, 1))),
                   pl.BlockSpec((D, TM), lambda i, j: (0, i))],
        out_shape=[jax.ShapeDtypeStruct((ROWS, 2 * D), BF16),
                   jax.ShapeDtypeStruct((D, ROWS), BF16)],
        compiler_params=_params(("parallel", "arbitrary")),
        name="diff_prep",
    )(z, *_rope_tables())


def _lambda_full(lam_ref, lambda_init):
    lf = lam_ref[...]
    a = jnp.sum(lf[0:1, :] * lf[1:2, :], axis=-1, keepdims=True)
    b = jnp.sum(lf[2:3, :] * lf[3:4, :], axis=-1, keepdims=True)
    return jnp.exp(a) - jnp.exp(b) + lambda_init


ATT_TQ = 1024
ATT_TK = 1024
ATT_QW = 256


def _split_maps(q):
    lane = lax.broadcasted_iota(jnp.int32, q.shape, 1)
    zero = jnp.zeros_like(q)
    return jnp.where(lane < DIFF_DH, q, zero), jnp.where(lane >= DIFF_DH, q, zero)


ATT_ONES = 16


ATT_CHAINS = [(mi, slice(w * ATT_QW, (w + 1) * ATT_QW))
              for mi in range(2) for w in range(ATT_TQ // ATT_QW)]


def _attn_scores(k, qz_ref, mi, sl):
    nt = (((1,), (1,)), ((), ()))
    return lax.dot_general(k, qz_ref[mi, sl, :], nt, preferred_element_type=F32)


def _attn_consume(s, vt1, m_ref, acc_ref, mi, sl):
    m_old = m_ref[mi, :, sl]
    m_new = jnp.maximum(m_old, jnp.max(s, axis=0, keepdims=True))
    alpha = jnp.exp2(m_old - m_new)
    m_ref[mi, :, sl] = m_new
    p = jnp.exp2((s - m_new).astype(BF16))
    acc_ref[mi, :, sl] = alpha * acc_ref[mi, :, sl] + jnp.dot(vt1, p, preferred_element_type=F32)


def _with_ones(vt):
    return jnp.concatenate([vt, jnp.ones((ATT_ONES, vt.shape[1]), BF16)], axis=0)


def _attn_lat_kernel(lam_ref, q_ref, kc_ref, vct_ref, k_ref, kn_ref, vt_ref, o_ref,
                     qz_ref, m_ref, acc_ref, s0_ref, s1_ref, *, lambda_init):
    j = pl.program_id(2)

    @pl.when(j == 0)
    def _():
        q1, q2 = _split_maps(q_ref[...])
        qz_ref[0] = q1
        qz_ref[1] = q2
        m_ref[...] = jnp.full_like(m_ref, -jnp.inf)
        acc_ref[...] = jnp.zeros_like(acc_ref)
        vct1 = _with_ones(vct_ref[...])
        for c, (mi, sl) in enumerate(ATT_CHAINS):
            s0_ref[c] = _attn_scores(k_ref[...], qz_ref, mi, sl)
            _attn_consume(_attn_scores(kc_ref[...], qz_ref, mi, sl), vct1, m_ref, acc_ref, mi, sl)

    def step(cur_ref, nxt_ref):
        vt1 = _with_ones(vt_ref[...])
        for c, (mi, sl) in enumerate(ATT_CHAINS):
            nxt_ref[c] = _attn_scores(kn_ref[...], qz_ref, mi, sl)
            _attn_consume(cur_ref[c], vt1, m_ref, acc_ref, mi, sl)

    parity = lax.rem(j, 2)
    pl.when(parity == 0)(lambda: step(s0_ref, s1_ref))
    pl.when(parity == 1)(lambda: step(s1_ref, s0_ref))

    @pl.when(j == pl.num_programs(2) - 1)
    def _():
        lmb = _lambda_full(lam_ref, lambda_init)
        num = lambda mi: acc_ref[mi, :HEAD_DIM, :]
        den = lambda mi: acc_ref[mi, HEAD_DIM:HEAD_DIM + 1, :]
        ot = num(0) / den(0) - lmb * (num(1) / den(1))
        o_ref[...] = ot.T


def _attn_ctx_kernel(lam_ref, q_ref, k_ref, vt_ref, o_ref, *, lambda_init):
    nt = (((1,), (1,)), ((), ()))
    outs = []
    for qz in _split_maps(q_ref[...]):
        s = lax.dot_general(k_ref[...], qz, nt, preferred_element_type=F32)
        p = jnp.exp2(s - jnp.max(s, axis=0, keepdims=True))
        o = jnp.dot(vt_ref[...], p.astype(BF16), preferred_element_type=F32)
        outs.append(o / jnp.sum(p, axis=0, keepdims=True))
    o_ref[...] = (outs[0] - _lambda_full(lam_ref, lambda_init) * outs[1]).T


def _diff_attention(qk, vt, lam, lambda_init):
    nq, nk = SEQ // ATT_TQ, SEQ // ATT_TK
    ctx_blk = LAT_ROWS // CTX
    lam_spec3 = pl.BlockSpec((4, DIFF_DH), lambda p, i, j: (0, 0))
    o_lat = pl.pallas_call(
        functools.partial(_attn_lat_kernel, lambda_init=lambda_init),
        grid=(BATCH * HEADS, nq, nk),
        in_specs=[lam_spec3,
                  pl.BlockSpec((ATT_TQ, HEAD_DIM), lambda p, i, j: ((p // HEADS) * nq + i, p % HEADS)),
                  pl.BlockSpec((CTX, HEAD_DIM), lambda p, i, j: (ctx_blk + p // HEADS, HEADS + p % HEADS)),
                  pl.BlockSpec((HEAD_DIM, CTX), lambda p, i, j: (p % HEADS, ctx_blk + p // HEADS)),
                  pl.BlockSpec((ATT_TK, HEAD_DIM), lambda p, i, j: ((p // HEADS) * nk + j, HEADS + p % HEADS)),
                  pl.BlockSpec((ATT_TK, HEAD_DIM),
                               lambda p, i, j: ((p // HEADS) * nk + jnp.minimum(j + 1, nk - 1), HEADS + p % HEADS)),
                  pl.BlockSpec((HEAD_DIM, ATT_TK), lambda p, i, j: (p % HEADS, (p // HEADS) * nk + j))],
        out_specs=pl.BlockSpec((ATT_TQ, HEAD_DIM), lambda p, i, j: ((p // HEADS) * nq + i, p % HEADS)),
        out_shape=jax.ShapeDtypeStruct((LAT_ROWS, D), F32),
        scratch_shapes=[pltpu.VMEM((2, ATT_TQ, HEAD_DIM), BF16),
                        pltpu.VMEM((2, 1, ATT_TQ), F32),
                        pltpu.VMEM((2, HEAD_DIM + ATT_ONES, ATT_TQ), F32),
                        pltpu.VMEM((len(ATT_CHAINS), ATT_TK, ATT_QW), F32),
                        pltpu.VMEM((len(ATT_CHAINS), ATT_TK, ATT_QW), F32)],
        compiler_params=_params(("parallel", "parallel", "arbitrary")),
        name="diff_attn_lat",
    )(lam, qk, qk, vt, qk, qk, vt)
    o_ctx = pl.pallas_call(
        functools.partial(_attn_ctx_kernel, lambda_init=lambda_init),
        grid=(BATCH, HEADS),
        in_specs=[pl.BlockSpec((4, DIFF_DH), lambda b, h: (0, 0)),
                  pl.BlockSpec((CTX, HEAD_DIM), lambda b, h: (ctx_blk + b, h)),
                  pl.BlockSpec((CTX, HEAD_DIM), lambda b, h: (ctx_blk + b, HEADS + h)),
                  pl.BlockSpec((HEAD_DIM, CTX), lambda b, h: (h, ctx_blk + b))],
        out_specs=pl.BlockSpec((CTX, HEAD_DIM), lambda b, h: (b, h)),
        out_shape=jax.ShapeDtypeStruct((CTX_ROWS, D), F32),
        compiler_params=_params(("parallel", "parallel")),
        name="diff_attn_ctx",
    )(lam, qk, qk, vt)
    return o_lat, o_ctx


def _diff_out_kernel(x_ref, ol_ref, oc_ref, ng_ref, gate_ref, w_ref, o_ref, *, out_scale):
    is_ctx = pl.program_id(0) >= N_LAT_TILES
    o = jnp.where(is_ctx, oc_ref[...], ol_ref[...])
    y = _head_rms(o, ng_ref[...]) * out_scale
    o_ref[...] = x_ref[...] + gate_ref[...] * jnp.dot(
        y.astype(BF16), w_ref[...], preferred_element_type=F32)


def _diff_layer(x, mods, g1, w_in, lam, norm_g, w_out, lambda_init):
    z = _proj_in(x, g1, mods, w_in.astype(BF16), tn=1024)
    o_lat, o_ctx = _diff_attention(*_diff_prep(z), lam, lambda_init)
    row = pl.BlockSpec((TM, D), lambda i: (i, 0))
    return pl.pallas_call(
        functools.partial(_diff_out_kernel, out_scale=1.0 - lambda_init),
        grid=(N_TILES,),
        in_specs=[row,
                  pl.BlockSpec((TM, D), lambda i: (jnp.minimum(i, N_LAT_TILES - 1), 0)),
                  pl.BlockSpec((TM, D), lambda i: (0, 0)),
                  pl.BlockSpec((1, HEAD_DIM), lambda i: (0, 0)),
                  _mod_spec(2),
                  pl.BlockSpec((D, D), lambda i: (0, 0))],
        out_specs=row,
        out_shape=jax.ShapeDtypeStruct((ROWS, D), F32),
        compiler_params=_params(("parallel",)),
        name="diff_out",
    )(x, o_lat, o_ctx, norm_g.reshape(1, HEAD_DIM), mods, w_out.astype(BF16))


def kernel(x, c, ctx, c_ctx, norm1_g, norm2_g, ada_w, ada_b, mlp_w1, mlp_w2, gdn_w_in, gdn_conv, gdn_a_log, gdn_dt_bias, gdn_norm_g, gdn_w_out, sconv_w_in, sconv_conv, sconv_w_out, diff_w_in, diff_lambda, diff_norm_g, diff_w_out, final_g):
    mods = _ada(c, c_ctx, ada_w, ada_b)
    h = jnp.concatenate([x.reshape(LAT_ROWS, D), ctx.reshape(CTX_ROWS, D)], axis=0)
    for i in range(DEPTH):
        kind, j = i % 3, i // 3
        if kind == 0:
            h = _gdn_layer(h, mods[i], norm1_g[i], gdn_w_in[j], gdn_conv[j], gdn_a_log[j],
                           gdn_dt_bias[j], gdn_norm_g[j], gdn_w_out[j])
        elif kind == 1:
            h = _sconv_layer(h, mods[i], norm1_g[i], sconv_w_in[j], sconv_conv[j], sconv_w_out[j])
        else:
            lambda_init = 0.8 - 0.6 * math.exp(-0.3 * i)
            h = _diff_layer(h, mods[i], norm1_g[i], diff_w_in[j], diff_lambda[j],
                            diff_norm_g[j], diff_w_out[j], lambda_init)
        h = _mlp(h, norm2_g[i], mods[i], mlp_w1[i].astype(BF16), mlp_w2[i].astype(BF16),
                 final_g, last=i == DEPTH - 1)
    return h.reshape(BATCH, SEQ, D)
```

```python
import functools
import math

import jax
import jax.numpy as jnp
from jax import lax
from jax.experimental import pallas as pl
from jax.experimental.pallas import tpu as pltpu

F32 = jnp.float32
BF16 = jnp.bfloat16

D = 1024
BATCH = 2
SEQ = 8192
CTX = 256
DEPTH = 4
N_ADA = 6
D_FF = 4 * D
EPS = 1e-6
HEADS = 8
HEAD_DIM = 128
CHUNK = 64
DIFF_DH = 64
GRID_W = 64
ROPE_BASE = 10000.0

LAT_ROWS = BATCH * SEQ
CTX_ROWS = BATCH * CTX
ROWS = LAT_ROWS + CTX_ROWS
TM = 512
N_TILES = ROWS // TM
N_LAT_TILES = LAT_ROWS // TM
TILES_PER_BATCH = SEQ // TM
VMEM_LIMIT = 56 * 1024 * 1024


def _params(sem, vmem=VMEM_LIMIT):
    return pltpu.CompilerParams(dimension_semantics=sem, vmem_limit_bytes=vmem)


def _mod_row(i):
    return jnp.where(i >= N_LAT_TILES, 0, 1 + i // TILES_PER_BATCH)


def _mod_spec(k):
    return pl.BlockSpec((None, None, 1, D), lambda i, *_: (_mod_row(i), k, 0, 0))


def _norm_mod(x, g, sh, sc):
    ms = jnp.mean(x * x, axis=-1, keepdims=True)
    return (x * lax.rsqrt(ms + EPS) * g) * (1.0 + sc) + sh


def _head_rms(o, g):
    outs = []
    for h in range(HEADS):
        oh = o[:, h * HEAD_DIM:(h + 1) * HEAD_DIM]
        ms = jnp.mean(oh * oh, axis=-1, keepdims=True)
        outs.append(oh * lax.rsqrt(ms + EPS) * g)
    return jnp.concatenate(outs, axis=1)


HALO = 8
BF16_HALO = 16


def _is_seq_start(r):
    starts = (0, SEQ, LAT_ROWS, LAT_ROWS + CTX)
    return functools.reduce(jnp.logical_or, [r == s for s in starts])


def _zero_row(a, row, cond):
    s0 = row // 8 * 8
    sub = lax.broadcasted_iota(jnp.int32, (8, 1), 0)
    slab = jnp.where(jnp.logical_and(sub == row - s0, cond), 0.0, a[s0:s0 + 8])
    parts = [a[:s0]] * (s0 > 0) + [slab] + [a[s0 + 8:]] * (s0 + 8 < a.shape[0])
    return jnp.concatenate(parts, axis=0)


def _conv3(xh, w, r0):
    n = xh.shape[0]
    tm = n - 2 * HALO
    x_dn = pltpu.roll(xh, 1, 0)[HALO:HALO + tm]
    x_up = pltpu.roll(xh, n - 1, 0)[HALO:HALO + tm]
    for c in range(0, tm, CTX):
        nxt = r0 + c + CTX
        x_dn = _zero_row(x_dn, c, _is_seq_start(r0 + c))
        x_up = _zero_row(x_up, c + CTX - 1, jnp.logical_or(_is_seq_start(nxt), nxt == ROWS))
    return x_dn * w[0:1, :] + xh[HALO:HALO + tm] * w[1:2, :] + x_up * w[2:3, :]


def _halo_specs(tm, tc, col_of, rows):
    nb = ROWS // rows
    prev = pl.BlockSpec((rows, tc), lambda i, *a: (jnp.maximum(i * (tm // rows) - 1, 0), col_of(i, *a)))
    nxt = pl.BlockSpec((rows, tc), lambda i, *a: (jnp.minimum((i + 1) * (tm // rows), nb - 1), col_of(i, *a)))
    return prev, nxt


def _ada_kernel(s_ref, w_ref, b_ref, o_ref):
    s = s_ref[...]
    s = s * jax.nn.sigmoid(s)
    o_ref[...] = jnp.dot(s.astype(BF16), w_ref[...].astype(BF16),
                         preferred_element_type=F32) + b_ref[...]


def _ada(c, c_ctx, ada_w, ada_b):
    tn = 1536
    s = jnp.concatenate([c_ctx[None, :], c, jnp.zeros((5, D), F32)], axis=0)
    out = pl.pallas_call(
        _ada_kernel,
        grid=(DEPTH, N_ADA * D // tn),
        in_specs=[pl.BlockSpec((8, D), lambda l, j: (0, 0)),
                  pl.BlockSpec((None, D, tn), lambda l, j: (l, 0, j)),
                  pl.BlockSpec((None, 1, tn), lambda l, j: (l, 0, j))],
        out_specs=pl.BlockSpec((None, 8, tn), lambda l, j: (l, 0, j)),
        out_shape=jax.ShapeDtypeStruct((DEPTH, 8, N_ADA * D), F32),
        compiler_params=_params(("parallel", "parallel")),
        name="ada",
    )(s, ada_w, ada_b.reshape(DEPTH, 1, N_ADA * D))
    return out[:, :3].reshape(DEPTH, 3, N_ADA, 1, D)


def _proj_in_kernel(x_ref, g_ref, sh_ref, sc_ref, w_ref, o_ref, h_ref):
    @pl.when(pl.program_id(1) == 0)
    def _():
        h_ref[...] = _norm_mod(x_ref[...], g_ref[...], sh_ref[...], sc_ref[...]).astype(BF16)

    o_ref[...] = jnp.dot(h_ref[...], w_ref[...], preferred_element_type=F32).astype(BF16)


def _proj_in(x, g, mods, w, tn):
    n = w.shape[1]
    return pl.pallas_call(
        _proj_in_kernel,
        grid=(N_TILES, n // tn),
        in_specs=[pl.BlockSpec((TM, D), lambda i, j: (i, 0)),
                  pl.BlockSpec((1, D), lambda i, j: (0, 0)),
                  _mod_spec(0), _mod_spec(1),
                  pl.BlockSpec((D, tn), lambda i, j: (0, j))],
        out_specs=pl.BlockSpec((TM, tn), lambda i, j: (i, j)),
        out_shape=jax.ShapeDtypeStruct((ROWS, n), BF16),
        scratch_shapes=[pltpu.VMEM((TM, D), BF16)],
        compiler_params=_params(("parallel", "arbitrary")),
        name="proj_in",
    )(x, g.reshape(1, D), mods, mods, w)


def _mlp_kernel(x_ref, g_ref, sh_ref, sc_ref, gate_ref, w1_ref, w2_ref, fg_ref, o_ref,
                h_ref, acc_ref, *, final_norm):
    j = pl.program_id(1)

    @pl.when(j == 0)
    def _():
        h_ref[...] = _norm_mod(x_ref[...], g_ref[...], sh_ref[...], sc_ref[...]).astype(BF16)
        acc_ref[...] = jnp.zeros_like(acc_ref)

    a = jnp.dot(h_ref[...], w1_ref[...], preferred_element_type=F32)
    a = jnp.square(jnp.maximum(a, 0.0))
    acc_ref[...] += jnp.dot(a.astype(BF16), w2_ref[...], preferred_element_type=F32)

    @pl.when(j == pl.num_programs(1) - 1)
    def _():
        y = x_ref[...] + gate_ref[...] * acc_ref[...]
        if final_norm:
            ms = jnp.mean(y * y, axis=-1, keepdims=True)
            y = y * lax.rsqrt(ms + EPS) * fg_ref[...]
        o_ref[...] = y


def _mlp(x, g, mods, w1, w2, final_g, *, last):
    tf = 2048
    n_tiles = N_LAT_TILES if last else N_TILES
    return pl.pallas_call(
        functools.partial(_mlp_kernel, final_norm=last),
        grid=(n_tiles, D_FF // tf),
        in_specs=[pl.BlockSpec((TM, D), lambda i, j: (i, 0)),
                  pl.BlockSpec((1, D), lambda i, j: (0, 0)),
                  _mod_spec(3), _mod_spec(4), _mod_spec(5),
                  pl.BlockSpec((D, tf), lambda i, j: (0, j)),
                  pl.BlockSpec((tf, D), lambda i, j: (j, 0)),
                  pl.BlockSpec((1, D), lambda i, j: (0, 0))],
        out_specs=pl.BlockSpec((TM, D), lambda i, j: (i, 0)),
        out_shape=jax.ShapeDtypeStruct((n_tiles * TM, D), F32),
        scratch_shapes=[pltpu.VMEM((TM, D), BF16), pltpu.VMEM((TM, D), F32)],
        compiler_params=_params(("parallel", "arbitrary")),
        name="mlp",
    )(x, g.reshape(1, D), mods, mods, mods, w1, w2, final_g.reshape(1, D))


GDN_HG = 8
GDN_GROUPS = HEADS // GDN_HG
GDN_GATE_COL = 3
GDN_GL = 4 * GDN_HG
GDN_GW = GDN_GROUPS * HEAD_DIM
GDN_SLAB = 256


def _gdn_proj_kernel(x_ref, xp_ref, xn_ref, g_ref, sh_ref, sc_ref, w_ref, wab_ref, cw_ref,
                     alog_ref, dtb_ref, o_ref, gates_ref, gates_t_ref, h_ref):
    i, j = pl.program_id(0), pl.program_id(1)
    rows = slice(HALO, HALO + TM)

    @pl.when(j == 0)
    def _():
        x = jnp.concatenate([xp_ref[...], x_ref[...], xn_ref[...]], axis=0)
        h_ref[...] = _norm_mod(x, g_ref[...], sh_ref[...], sc_ref[...]).astype(BF16)
        ab = jnp.dot(h_ref[...], wab_ref[...], preferred_element_type=F32)[rows]
        lane = lax.broadcasted_iota(jnp.int32, ab.shape, 1) & 3
        t = ab + dtb_ref[...]
        softplus = jnp.maximum(t, 0.0) + jnp.log1p(jnp.exp(-jnp.abs(t)))
        gates = jnp.where(lane < 2, -jnp.exp(alog_ref[...]) * softplus, jax.nn.sigmoid(ab))
        gates_ref[...] = gates
        for hg in range(GDN_GROUPS):
            gt = gates[:, hg * HEAD_DIM:(hg + 1) * HEAD_DIM].T
            gates_t_ref[hg * GDN_GL:(hg + 1) * GDN_GL, :] = gt[0:GDN_GL, :]

    def columns(mode):
        scale = jnp.where(j == 0, HEAD_DIM ** -0.5, 1.0).astype(F32)
        for c0 in range(0, D, GDN_SLAB):
            cs = slice(c0, c0 + GDN_SLAB)
            z = jnp.dot(h_ref[...], w_ref[:, cs], preferred_element_type=F32)
            if mode == 'gate':
                o_ref[:, cs] = z[rows].astype(BF16)
                continue
            y = _conv3(z, cw_ref[:, cs], i * TM)
            y = y * jax.nn.sigmoid(y)
            if mode == 'v':
                o_ref[:, cs] = y.astype(BF16)
                continue
            for h0 in range(0, GDN_SLAB, HEAD_DIM):
                yh = y[:, h0:h0 + HEAD_DIM]
                ss = jnp.sum(yh * yh, axis=-1, keepdims=True)
                o_ref[:, c0 + h0:c0 + h0 + HEAD_DIM] = (yh * (lax.rsqrt(ss + EPS) * scale)).astype(BF16)

    pl.when(j < 2)(lambda: columns('qk'))
    pl.when(j == 2)(lambda: columns('v'))
    pl.when(j == 3)(lambda: columns('gate'))


def _gdn_proj(x, g, mods, w, wab, conv_w, alog_pad, dtb_pad):
    prev, nxt = _halo_specs(TM, D, lambda i, j: 0, HALO)
    full = lambda shape: pl.BlockSpec(shape, lambda i, j: (0, 0))
    return pl.pallas_call(
        _gdn_proj_kernel,
        grid=(N_TILES, 4),
        in_specs=[pl.BlockSpec((TM, D), lambda i, j: (i, 0)), prev, nxt,
                  full((1, D)), _mod_spec(0), _mod_spec(1),
                  pl.BlockSpec((D, D), lambda i, j: (0, j)),
                  full((D, GDN_GW)),
                  pl.BlockSpec((3, D), lambda i, j: (0, jnp.minimum(j, 2))),
                  full((1, GDN_GW)), full((1, GDN_GW))],
        out_specs=[pl.BlockSpec((TM, D), lambda i, j: (i, j)),
                   pl.BlockSpec((TM, GDN_GW), lambda i, j: (i, 0)),
                   pl.BlockSpec((GDN_GROUPS * GDN_GL, TM), lambda i, j: (0, i))],
        out_shape=[jax.ShapeDtypeStruct((ROWS, 4 * D), BF16),
                   jax.ShapeDtypeStruct((ROWS, GDN_GW), F32),
                   jax.ShapeDtypeStruct((GDN_GROUPS * GDN_GL, ROWS), F32)],
        scratch_shapes=[pltpu.VMEM((TM + 2 * HALO, D), BF16)],
        compiler_params=_params(("parallel", "arbitrary")),
        name="gdn_proj",
    )(x, x, x, g.reshape(1, D), mods, mods, w, wab, conv_w, alog_pad, dtb_pad)


GDN_TB = 256
GDN_NC = GDN_TB // CHUNK
GDN_STEPS = (CTX + SEQ) // GDN_TB
GDN_CTX_BLK = LAT_ROWS // GDN_TB
GDN_LAT_BLKS = SEQ // GDN_TB


def _split_dot(a_exact, x, *, x_is_lhs=False):
    hi = x.astype(BF16)
    r1 = x - hi.astype(F32)
    mid = r1.astype(BF16)
    lo = (r1 - mid.astype(F32)).astype(BF16)
    out = None
    for part in (hi, mid, lo):
        t = (jnp.dot(part, a_exact, preferred_element_type=F32) if x_is_lhs
             else jnp.dot(a_exact, part, preferred_element_type=F32))
        out = t if out is None else out + t
    return out


TRI_BASE_SHIFT = 3


def _tri_masks(r, c):
    same = lambda shift: (r >> shift) == (c >> shift)
    offs = [same(shift + 1) & jnp.logical_not(same(shift)) for shift in range(TRI_BASE_SHIFT, 6)]
    return same(TRI_BASE_SHIFT), offs


GDN_PACK = GDN_NC


def _block_diag(b16, lbm):
    zero = jnp.zeros_like(b16)
    return jnp.concatenate([jnp.where(m, b16, zero) for m in lbm], axis=0)


def _pdot(a, b, lbm):
    return jnp.dot(a.astype(BF16), _block_diag(b.astype(BF16), lbm), preferred_element_type=F32)


def _lane_blocks(lbm, parts):
    out = parts[-1]
    for m, part in zip(lbm[-2::-1], parts[-2::-1]):
        out = jnp.where(m, part, out)
    return out


def _tri_inverse_packed(lms, eye, masks, lbm):
    base, offs = masks
    lds = [jnp.where(base, lm, 0.0) for lm in lms]
    ps = [eye - ld for ld in lds]
    qs = [_pdot(ld, ld, lbm) for ld in lds]
    pqs = [_pdot(jnp.concatenate([q, p], axis=0), q, lbm) for q, p in zip(qs, ps)]
    ts = [p + pq[CHUNK:] for p, pq in zip(ps, pqs)]
    ts = [t + _pdot(t, pq[:CHUNK], lbm) for t, pq in zip(ts, pqs)]
    for off in offs:
        xs = [_pdot(jnp.where(off, lm, 0.0), t, lbm) for lm, t in zip(lms, ts)]
        ts = [t - _pdot(t, x, lbm) for t, x in zip(ts, xs)]
    return ts


def _gdn_scan_kernel(qf_ref, kf_ref, vf_ref, gf_ref, gtf_ref,
                     qb_ref, kb_ref, vb_ref, gb_ref, gtb_ref,
                     of_ref, ob_ref, s_ref):
    @pl.when(pl.program_id(1) == 0)
    def _():
        s_ref[...] = jnp.zeros_like(s_ref)

    r = lax.broadcasted_iota(jnp.int32, (CHUNK, GDN_TB), 0)
    lane = lax.broadcasted_iota(jnp.int32, (CHUNK, GDN_TB), 1)
    c = lane & (CHUNK - 1)
    lbm = [(lane >> 6) == i for i in range(GDN_PACK)]
    eye = (r == c).astype(F32)
    tri_masks = _tri_masks(r, c)
    r2 = lax.broadcasted_iota(jnp.int32, (GDN_TB, GDN_TB), 0)
    c2 = lax.broadcasted_iota(jnp.int32, (GDN_TB, GDN_TB), 1)
    same = (r2 >> 6) == (c2 >> 6)
    lower = jnp.where(same & (r2 >= c2), 1.0, 0.0).astype(BF16)
    upper = jnp.where(same & (r2 <= c2), 1.0, 0.0).astype(BF16)
    nt = (((1,), (1,)), ((), ()))
    tn = (((0,), (0,)), ((), ()))
    chunk = lambda a, ci: a[ci * CHUNK:(ci + 1) * CHUNK]
    zero_k = jnp.zeros((CHUNK, HEAD_DIM), BF16)

    dirs = ((qf_ref, kf_ref, vf_ref, gf_ref, gtf_ref, of_ref),
            (qb_ref, kb_ref, vb_ref, gb_ref, gtb_ref, ob_ref))
    streams = []
    for d, (q_ref, k_ref, v_ref, g_ref, gt_ref, o_ref) in enumerate(dirs):
        fwd = d == 0
        g_all = g_ref[...]
        gc_cols = _split_dot(lower if fwd else upper, g_all)
        gc_rows = _split_dot(upper if fwd else lower, gt_ref[...], x_is_lhs=True)
        last = CHUNK - 1 if fwd else 0
        for hl in range(GDN_HG):
            l0 = hl * HEAD_DIM
            gl = 4 * hl + d
            gcol = gc_cols[:, gl:gl + 1]
            glasts = [chunk(gcol, ci)[last:last + 1, :] for ci in range(GDN_PACK)]
            streams.append(dict(
                d=d, hl=hl, l0=l0, o_ref=o_ref,
                incl=(r >= c) if fwd else (r <= c),
                strict=(r > c) if fwd else (r < c),
                q=q_ref[:, l0:l0 + HEAD_DIM], k=k_ref[:, l0:l0 + HEAD_DIM], v=v_ref[:, l0:l0 + HEAD_DIM],
                gcol=gcol, glasts=glasts,
                beta=g_all[:, gl + 2:gl + 3],
                grow=gc_rows[gl:gl + 1, :]))

    for s in streams:
        gcol_p = _lane_blocks(lbm, [chunk(s['gcol'], ci) for ci in range(GDN_PACK)])
        s['decay'] = jnp.where(s['incl'], jnp.exp(jnp.where(s['incl'], gcol_p - s['grow'], 0.0)), 0.0)
        lhs = jnp.concatenate([jnp.concatenate([chunk(s['k'], ci), chunk(s['q'], ci)], axis=0)
                               for ci in range(GDN_PACK)], axis=1)
        rhs = jnp.concatenate([jnp.concatenate([chunk(s['k'], ci) if cj == ci else zero_k
                                                for cj in range(GDN_PACK)], axis=1)
                               for ci in range(GDN_PACK)], axis=0)
        s['kkqk'] = lax.dot_general(lhs, rhs, nt, preferred_element_type=F32)
    for s in streams:
        beta_p = _lane_blocks(lbm, [chunk(s['beta'], ci) for ci in range(GDN_PACK)])
        s['lm'] = jnp.where(s['strict'], beta_p * s['kkqk'][:CHUNK] * s['decay'], 0.0)
        s['attn'] = jnp.where(s['incl'], s['kkqk'][CHUNK:] * s['decay'], 0.0).astype(BF16)
        eg = jnp.exp(s['gcol'])
        glast_rows = jnp.concatenate([jnp.broadcast_to(g, (CHUNK, 1)) for g in s['glasts']], axis=0)
        q, k, v = (s[n].astype(F32) for n in 'qkv')
        s['qd'] = q * eg
        s['rhs'] = jnp.concatenate([v * s['beta'], k * (s['beta'] * eg)], axis=1).astype(BF16)
        s['kt'] = (k * jnp.exp(glast_rows - s['gcol'])).astype(BF16)
    tinvs = _tri_inverse_packed([s['lm'] for s in streams], eye, tri_masks, lbm)
    zero_t = jnp.zeros((CHUNK, GDN_TB), BF16)
    for s, tinv in zip(streams, tinvs):
        t16 = tinv.astype(BF16)
        s['uws'] = [jnp.dot(jnp.where(m, t16, zero_t), s['rhs'], preferred_element_type=F32).astype(BF16)
                    for m in lbm]
    for s in streams:
        uw_all = jnp.concatenate(s['uws'], axis=0)
        s['au_aw'] = [jnp.dot(jnp.where(m, s['attn'], zero_t), uw_all, preferred_element_type=F32)
                      for m in lbm]
    probs = []
    for s in streams:
        for ci in range(GDN_PACK):
            au_aw = s['au_aw'][ci]
            ku_kw = lax.dot_general(chunk(s['kt'], ci), s['uws'][ci], tn, preferred_element_type=F32)
            probs.append(dict(
                d=s['d'], hl=s['hl'], ci=ci, l0=s['l0'], o_ref=s['o_ref'],
                rows=slice(ci * CHUNK, (ci + 1) * CHUNK), glast=s['glasts'][ci],
                o0=au_aw[:, :HEAD_DIM], n=ku_kw[:, :HEAD_DIM],
                kq=jnp.concatenate([ku_kw[:, HEAD_DIM:], chunk(s['qd'], ci) - au_aw[:, HEAD_DIM:]],
                                   axis=0).astype(BF16)))

    states = [s_ref[i] for i in range(2 * GDN_HG)]
    for step in range(GDN_NC):
        for p in probs:
            if p['ci'] != (step if p['d'] == 0 else GDN_NC - 1 - step):
                continue
            si = p['d'] * GDN_HG + p['hl']
            ks_qs = jnp.dot(p['kq'], states[si].astype(BF16), preferred_element_type=F32)
            p['o_ref'][p['rows'], p['l0']:p['l0'] + HEAD_DIM] = (ks_qs[HEAD_DIM:] + p['o0']).astype(BF16)
            states[si] = states[si] * jnp.exp(p['glast']) - ks_qs[:HEAD_DIM] + p['n']
    for i in range(2 * GDN_HG):
        s_ref[i] = states[i]


def _gdn_scan(qkvg, gates, gates_t):
    ngroups = GDN_GROUPS
    hw = GDN_HG * HEAD_DIM

    def blk(p, t, bwd):
        b = p // ngroups
        lat = b * GDN_LAT_BLKS + (GDN_LAT_BLKS - t if bwd else t - 1)
        return jnp.where(t == 0, GDN_CTX_BLK + b, lat)

    def specs(bwd):
        col = lambda base: (lambda p, t: (blk(p, t, bwd), base * ngroups + p % ngroups))
        return [pl.BlockSpec((GDN_TB, hw), col(0)),
                pl.BlockSpec((GDN_TB, hw), col(1)),
                pl.BlockSpec((GDN_TB, hw), col(2)),
                pl.BlockSpec((GDN_TB, HEAD_DIM), lambda p, t: (blk(p, t, bwd), p % ngroups)),
                pl.BlockSpec((GDN_GL, GDN_TB), lambda p, t: (p % ngroups, blk(p, t, bwd)))]

    out_spec = lambda bwd: pl.BlockSpec((GDN_TB, hw), lambda p, t: (blk(p, t, bwd), p % ngroups))
    return pl.pallas_call(
        _gdn_scan_kernel,
        grid=(BATCH * ngroups, GDN_STEPS),
        in_specs=specs(False) + specs(True),
        out_specs=[out_spec(False), out_spec(True)],
        out_shape=[jax.ShapeDtypeStruct((ROWS, D), BF16)] * 2,
        scratch_shapes=[pltpu.VMEM((2 * GDN_HG, HEAD_DIM, HEAD_DIM), F32)],
        compiler_params=_params(("parallel", "arbitrary")),
        name="gdn_scan",
    )(qkvg, qkvg, qkvg, gates, gates_t, qkvg, qkvg, qkvg, gates, gates_t)


def _gdn_out_kernel(x_ref, of_ref, ob_ref, zg_ref, ng_ref, gate_ref, w_ref, o_ref):
    zg = zg_ref[...].astype(F32)
    o = of_ref[...].astype(F32) + ob_ref[...].astype(F32)
    y = _head_rms(o, ng_ref[...]) * (zg * jax.nn.sigmoid(zg))
    o_ref[...] = x_ref[...] + gate_ref[...] * jnp.dot(
        y.astype(BF16), w_ref[...], preferred_element_type=F32)


def _gdn_out(x, o_f, o_b, z, norm_g, mods, w_out):
    row = pl.BlockSpec((TM, D), lambda i: (i, 0))
    return pl.pallas_call(
        _gdn_out_kernel,
        grid=(N_TILES,),
        in_specs=[row, row, row,
                  pl.BlockSpec((TM, D), lambda i: (i, GDN_GATE_COL)),
                  pl.BlockSpec((1, HEAD_DIM), lambda i: (0, 0)),
                  _mod_spec(2),
                  pl.BlockSpec((D, D), lambda i: (0, 0))],
        out_specs=row,
        out_shape=jax.ShapeDtypeStruct((ROWS, D), F32),
        compiler_params=_params(("parallel",)),
        name="gdn_out",
    )(x, o_f, o_b, z, norm_g.reshape(1, HEAD_DIM), mods, w_out)


def _gdn_layer(x, mods, g1, w_in, conv_w, a_log, dt_bias, norm_g, w_out):
    ab = w_in[:, 4 * D:].reshape(D, 2, 2, HEADS)
    ab = jnp.transpose(ab, (0, 3, 1, 2)).reshape(D, GDN_GROUPS, GDN_GL)
    wab = jnp.pad(ab, ((0, 0), (0, 0), (0, HEAD_DIM - GDN_GL))).reshape(D, GDN_GW).astype(BF16)

    def pad_head_param(p):
        p = jnp.pad(jnp.transpose(p, (1, 0)), ((0, 0), (0, 2))).reshape(GDN_GROUPS, GDN_GL)
        return jnp.pad(p, ((0, 0), (0, HEAD_DIM - GDN_GL))).reshape(1, GDN_GW).astype(F32)

    qkvg, gates, gates_t = _gdn_proj(x, g1, mods, w_in[:, :4 * D].astype(BF16), wab, conv_w,
                                     pad_head_param(a_log), pad_head_param(dt_bias))
    o_f, o_b = _gdn_scan(qkvg, gates, gates_t)
    return _gdn_out(x, o_f, o_b, qkvg, norm_g, mods, w_out.astype(BF16))


def _sconv_out_kernel(x_ref, bg_ref, cg_ref, hv_ref, cgp_ref, hvp_ref, cgn_ref, hvn_ref,
                      cw_ref, gate_ref, w_ref, o_ref):
    f32 = lambda ref: ref[...].astype(F32)
    prev = (f32(cgp_ref) * f32(hvp_ref))[BF16_HALO - HALO:]
    nxt = (f32(cgn_ref) * f32(hvn_ref))[:HALO]
    xh = jnp.concatenate([prev, f32(cg_ref) * f32(hv_ref), nxt], axis=0)
    y = f32(bg_ref) * _conv3(xh, cw_ref[...], pl.program_id(0) * TM)
    o_ref[...] = x_ref[...] + gate_ref[...] * jnp.dot(
        y.astype(BF16), w_ref[...], preferred_element_type=F32)


def _sconv_layer(x, mods, g1, w_in, conv_w, w_out):
    z = _proj_in(x, g1, mods, w_in.astype(BF16), tn=1024)
    colspec = lambda cidx: pl.BlockSpec((TM, D), lambda i: (i, cidx))
    p1, n1 = _halo_specs(TM, D, lambda i: 1, BF16_HALO)
    p2, n2 = _halo_specs(TM, D, lambda i: 2, BF16_HALO)
    return pl.pallas_call(
        _sconv_out_kernel,
        grid=(N_TILES,),
        in_specs=[colspec(0), colspec(0), colspec(1), colspec(2), p1, p2, n1, n2,
                  pl.BlockSpec((3, D), lambda i: (0, 0)),
                  _mod_spec(2),
                  pl.BlockSpec((D, D), lambda i: (0, 0))],
        out_specs=colspec(0),
        out_shape=jax.ShapeDtypeStruct((ROWS, D), F32),
        compiler_params=_params(("parallel",)),
        name="sconv_out",
    )(x, z, z, z, z, z, z, z, conv_w, mods, w_out.astype(BF16))


def _rope_tables():
    t = jnp.arange(SEQ)
    row = (t // GRID_W).astype(F32)
    colp = (t % GRID_W).astype(F32)
    nf = DIFF_DH // 4
    inv = ROPE_BASE ** (-jnp.arange(nf, dtype=F32) / nf)
    ang = jnp.concatenate([row[:, None] * inv, colp[:, None] * inv], axis=-1)
    ang = jnp.tile(jnp.repeat(ang, 2, axis=-1), (1, HEAD_DIM // DIFF_DH))
    even = (jnp.arange(HEAD_DIM) % 2) == 0
    cos, sin = jnp.cos(ang), jnp.sin(ang)
    return cos, jnp.where(even, -sin, 0.0), jnp.where(even, 0.0, sin)


Q_SCALE = DIFF_DH ** -0.5 * math.log2(math.e)


def _diff_prep_kernel(z_ref, cos_ref, sa_ref, sb_ref, qk_ref, vt_ref):
    i, j = pl.program_id(0), pl.program_id(1)
    scale = jnp.where(j == 0, Q_SCALE, 1.0).astype(F32)

    @pl.when(jnp.logical_and(j < 2, i < N_LAT_TILES))
    def _():
        cos, sa, sb = cos_ref[...], sa_ref[...], sb_ref[...]
        for h in range(HEADS):
            sl = slice(h * HEAD_DIM, (h + 1) * HEAD_DIM)
            x = z_ref[:, sl].astype(F32)
            y = x * cos + pltpu.roll(x, HEAD_DIM - 1, 1) * sa + pltpu.roll(x, 1, 1) * sb
            qk_ref[:, sl] = (y * scale).astype(BF16)

    @pl.when(jnp.logical_and(j < 2, i >= N_LAT_TILES))
    def _():
        qk_ref[...] = (z_ref[...].astype(F32) * scale).astype(BF16)

    @pl.when(j == 2)
    def _():
        vt_ref[...] = z_ref[...].astype(F32).T.astype(BF16)


def _diff_prep(z):
    tab = pl.BlockSpec((TM, HEAD_DIM), lambda i, j: (jnp.minimum(i, N_LAT_TILES - 1) % TILES_PER_BATCH, 0))
    return pl.pallas_call(
        _diff_prep_kernel,
        grid=(N_TILES, 3),
        in_specs=[pl.BlockSpec((TM, D), lambda i, j: (i, j)), tab, tab, tab],
        out_specs=[pl.BlockSpec((TM, D), lambda i, j: (i, jnp.minimum(j, 1))),
                   pl.BlockSpec((D, TM), lambda i, j: (0, i))],
        out_shape=[jax.ShapeDtypeStruct((ROWS, 2 * D), BF16),
                   jax.ShapeDtypeStruct((D, ROWS), BF16)],
        compiler_params=_params(("parallel", "arbitrary")),
        name="diff_prep",
    )(z, *_rope_tables())


def _lambda_full(lam_ref, lambda_init):
    lf = lam_ref[...]
    a = jnp.sum(lf[0:1, :] * lf[1:2, :], axis=-1, keepdims=True)
    b = jnp.sum(lf[2:3, :] * lf[3:4, :], axis=-1, keepdims=True)
    return jnp.exp(a) - jnp.exp(b) + lambda_init


ATT_TQ = 2048
ATT_TK = 1024
ATT_QW = 256


def _split_maps(q):
    lane = lax.broadcasted_iota(jnp.int32, q.shape, 1)
    zero = jnp.zeros_like(q)
    return jnp.where(lane < DIFF_DH, q, zero), jnp.where(lane >= DIFF_DH, q, zero)


ATT_ONES = 16


ATT_CHAINS = [(mi, slice(w * ATT_QW, (w + 1) * ATT_QW))
              for mi in range(2) for w in range(ATT_TQ // ATT_QW)]


def _attn_scores(k, qz_ref, mi, sl):
    nt = (((1,), (1,)), ((), ()))
    return lax.dot_general(k, qz_ref[mi, sl, :], nt, preferred_element_type=F32)


def _attn_consume(s, s_max, vt1, m_ref, acc_ref, mi, sl):
    m_old = m_ref[mi, :, sl]
    m_new = jnp.maximum(m_old, s_max)
    alpha = jnp.exp2(m_old - m_new)
    m_ref[mi, :, sl] = m_new
    p = jnp.exp2((s - m_new).astype(BF16))
    acc_ref[mi, :, sl] = alpha * acc_ref[mi, :, sl] + jnp.dot(vt1, p, preferred_element_type=F32)


def _with_ones(vt):
    return jnp.concatenate([vt, jnp.ones((ATT_ONES, vt.shape[1]), BF16)], axis=0)


def _attn_lat_kernel(lam_ref, q_ref, kc_ref, vct_ref, k_ref, vt_ref, o_ref,
                     qz_ref, m_ref, acc_ref, s0_ref, s1_ref, x0_ref, x1_ref, *, lambda_init, nk):
    j = pl.program_id(2)
    bufs = ((s0_ref, x0_ref), (s1_ref, x1_ref))

    def produce(buf, c, mi, sl):
        s = _attn_scores(k_ref[...], qz_ref, mi, sl)
        buf[0][c] = s
        buf[1][c] = jnp.max(s, axis=0, keepdims=True)

    @pl.when(j == 0)
    def _():
        q1, q2 = _split_maps(q_ref[...])
        qz_ref[0] = q1
        qz_ref[1] = q2
        m_ref[...] = jnp.full_like(m_ref, -jnp.inf)
        acc_ref[...] = jnp.zeros_like(acc_ref)
        vct1 = _with_ones(vct_ref[...])
        for c, (mi, sl) in enumerate(ATT_CHAINS):
            produce(bufs[0], c, mi, sl)
            s = _attn_scores(kc_ref[...], qz_ref, mi, sl)
            _attn_consume(s, jnp.max(s, axis=0, keepdims=True), vct1, m_ref, acc_ref, mi, sl)

    def step(cur, nxt):
        vt1 = _with_ones(vt_ref[...])
        for c, (mi, sl) in enumerate(ATT_CHAINS):
            if nxt is not None:
                produce(nxt, c, mi, sl)
            _attn_consume(cur[0][c], cur[1][c], vt1, m_ref, acc_ref, mi, sl)

    middle = jnp.logical_and(j > 0, j < nk)
    odd = lax.rem(j, 2) == 1
    pl.when(jnp.logical_and(middle, odd))(lambda: step(bufs[0], bufs[1]))
    pl.when(jnp.logical_and(middle, jnp.logical_not(odd)))(lambda: step(bufs[1], bufs[0]))

    @pl.when(j == nk)
    def _():
        step(bufs[(nk - 1) % 2], None)
        lmb = _lambda_full(lam_ref, lambda_init)
        num = lambda mi: acc_ref[mi, :HEAD_DIM, :]
        den = lambda mi: acc_ref[mi, HEAD_DIM:HEAD_DIM + 1, :]
        ot = num(0) / den(0) - lmb * (num(1) / den(1))
        o_ref[...] = ot.T


def _attn_ctx_kernel(lam_ref, q_ref, k_ref, vt_ref, o_ref, *, lambda_init):
    nt = (((1,), (1,)), ((), ()))
    outs = []
    for qz in _split_maps(q_ref[...]):
        s = lax.dot_general(k_ref[...], qz, nt, preferred_element_type=F32)
        p = jnp.exp2(s - jnp.max(s, axis=0, keepdims=True))
        o = jnp.dot(vt_ref[...], p.astype(BF16), preferred_element_type=F32)
        outs.append(o / jnp.sum(p, axis=0, keepdims=True))
    o_ref[...] = (outs[0] - _lambda_full(lam_ref, lambda_init) * outs[1]).T


def _diff_attention(qk, vt, lam, lambda_init):
    nq, nk = SEQ // ATT_TQ, SEQ // ATT_TK
    ctx_blk = LAT_ROWS // CTX
    lam_spec3 = pl.BlockSpec((4, DIFF_DH), lambda p, i, j: (0, 0))
    o_lat = pl.pallas_call(
        functools.partial(_attn_lat_kernel, lambda_init=lambda_init, nk=nk),
        grid=(BATCH * HEADS, nq, nk + 1),
        in_specs=[lam_spec3,
                  pl.BlockSpec((ATT_TQ, HEAD_DIM), lambda p, i, j: ((p // HEADS) * nq + i, p % HEADS)),
                  pl.BlockSpec((CTX, HEAD_DIM), lambda p, i, j: (ctx_blk + p // HEADS, HEADS + p % HEADS)),
                  pl.BlockSpec((HEAD_DIM, CTX), lambda p, i, j: (p % HEADS, ctx_blk + p // HEADS)),
                  pl.BlockSpec((ATT_TK, HEAD_DIM),
                               lambda p, i, j: ((p // HEADS) * nk + jnp.minimum(j, nk - 1), HEADS + p % HEADS)),
                  pl.BlockSpec((HEAD_DIM, ATT_TK),
                               lambda p, i, j: (p % HEADS, (p // HEADS) * nk + jnp.maximum(j - 1, 0)))],
        out_specs=pl.BlockSpec((ATT_TQ, HEAD_DIM), lambda p, i, j: ((p // HEADS) * nq + i, p % HEADS)),
        out_shape=jax.ShapeDtypeStruct((LAT_ROWS, D), F32),
        scratch_shapes=[pltpu.VMEM((2, ATT_TQ, HEAD_DIM), BF16),
                        pltpu.VMEM((2, 1, ATT_TQ), F32),
                        pltpu.VMEM((2, HEAD_DIM + ATT_ONES, ATT_TQ), F32),
                        pltpu.VMEM((len(ATT_CHAINS), ATT_TK, ATT_QW), F32),
                        pltpu.VMEM((len(ATT_CHAINS), ATT_TK, ATT_QW), F32),
                        pltpu.VMEM((len(ATT_CHAINS), 1, ATT_QW), F32),
                        pltpu.VMEM((len(ATT_CHAINS), 1, ATT_QW), F32)],
        compiler_params=_params(("parallel", "parallel", "arbitrary")),
        name="diff_attn_lat",
    )(lam, qk, qk, vt, qk, vt)
    o_ctx = pl.pallas_call(
        functools.partial(_attn_ctx_kernel, lambda_init=lambda_init),
        grid=(BATCH, HEADS),
        in_specs=[pl.BlockSpec((4, DIFF_DH), lambda b, h: (0, 0)),
                  pl.BlockSpec((CTX, HEAD_DIM), lambda b, h: (ctx_blk + b, h)),
                  pl.BlockSpec((CTX, HEAD_DIM), lambda b, h: (ctx_blk + b, HEADS + h)),
                  pl.BlockSpec((HEAD_DIM, CTX), lambda b, h: (h, ctx_blk + b))],
        out_specs=pl.BlockSpec((CTX, HEAD_DIM), lambda b, h: (b, h)),
        out_shape=jax.ShapeDtypeStruct((CTX_ROWS, D), F32),
        compiler_params=_params(("parallel", "parallel")),
        name="diff_attn_ctx",
    )(lam, qk, qk, vt)
    return o_lat, o_ctx


def _diff_out_kernel(x_ref, ol_ref, oc_ref, ng_ref, gate_ref, w_ref, o_ref, *, out_scale):
    is_ctx = pl.program_id(0) >= N_LAT_TILES
    o = jnp.where(is_ctx, oc_ref[...], ol_ref[...])
    y = _head_rms(o, ng_ref[...]) * out_scale
    o_ref[...] = x_ref[...] + gate_ref[...] * jnp.dot(
        y.astype(BF16), w_ref[...], preferred_element_type=F32)


def _diff_layer(x, mods, g1, w_in, lam, norm_g, w_out, lambda_init):
    z = _proj_in(x, g1, mods, w_in.astype(BF16), tn=1024)
    o_lat, o_ctx = _diff_attention(*_diff_prep(z), lam, lambda_init)
    row = pl.BlockSpec((TM, D), lambda i: (i, 0))
    return pl.pallas_call(
        functools.partial(_diff_out_kernel, out_scale=1.0 - lambda_init),
        grid=(N_TILES,),
        in_specs=[row,
                  pl.BlockSpec((TM, D), lambda i: (jnp.minimum(i, N_LAT_TILES - 1), 0)),
                  pl.BlockSpec((TM, D), lambda i: (0, 0)),
                  pl.BlockSpec((1, HEAD_DIM), lambda i: (0, 0)),
                  _mod_spec(2),
                  pl.BlockSpec((D, D), lambda i: (0, 0))],
        out_specs=row,
        out_shape=jax.ShapeDtypeStruct((ROWS, D), F32),
        compiler_params=_params(("parallel",)),
        name="diff_out",
    )(x, o_lat, o_ctx, norm_g.reshape(1, HEAD_DIM), mods, w_out.astype(BF16))


def kernel(x, c, ctx, c_ctx, norm1_g, norm2_g, ada_w, ada_b, mlp_w1, mlp_w2, gdn_w_in, gdn_conv, gdn_a_log, gdn_dt_bias, gdn_norm_g, gdn_w_out, sconv_w_in, sconv_conv, sconv_w_out, diff_w_in, diff_lambda, diff_norm_g, diff_w_out, final_g):
    mods = _ada(c, c_ctx, ada_w, ada_b)
    h = jnp.concatenate([x.reshape(LAT_ROWS, D), ctx.reshape(CTX_ROWS, D)], axis=0)
    for i in range(DEPTH):
        kind, j = i % 3, i // 3
        if kind == 0:
            h = _gdn_layer(h, mods[i], norm1_g[i], gdn_w_in[j], gdn_conv[j], gdn_a_log[j],
                           gdn_dt_bias[j], gdn_norm_g[j], gdn_w_out[j])
        elif kind == 1:
            h = _sconv_layer(h, mods[i], norm1_g[i], sconv_w_in[j], sconv_conv[j], sconv_w_out[j])
        else:
            lambda_init = 0.8 - 0.6 * math.exp(-0.3 * i)
            h = _diff_layer(h, mods[i], norm1_g[i], diff_w_in[j], diff_lambda[j],
                            diff_norm_g[j], diff_w_out[j], lambda_init)
        h = _mlp(h, norm2_g[i], mods[i], mlp_w1[i].astype(BF16), mlp_w2[i].astype(BF16),
                 final_g, last=i == DEPTH - 1)
    return h.reshape(BATCH, SEQ, D)
```

```python
import functools
import math

import jax
import jax.numpy as jnp
from jax import lax
from jax.experimental import pallas as pl
from jax.experimental.pallas import tpu as pltpu

F32 = jnp.float32
BF16 = jnp.bfloat16

D = 1024
BATCH = 2
SEQ = 8192
CTX = 256
DEPTH = 4
N_ADA = 6
D_FF = 4 * D
EPS = 1e-6
HEADS = 8
HEAD_DIM = 128
CHUNK = 64
DIFF_DH = 64
GRID_W = 64
ROPE_BASE = 10000.0

LAT_ROWS = BATCH * SEQ
CTX_ROWS = BATCH * CTX
ROWS = LAT_ROWS + CTX_ROWS
TM = 512
N_TILES = ROWS // TM
N_LAT_TILES = LAT_ROWS // TM
TILES_PER_BATCH = SEQ // TM
VMEM_LIMIT = 56 * 1024 * 1024


def _params(sem, vmem=VMEM_LIMIT):
    return pltpu.CompilerParams(dimension_semantics=sem, vmem_limit_bytes=vmem)


def _mod_row(i):
    return jnp.where(i >= N_LAT_TILES, 0, 1 + i // TILES_PER_BATCH)


def _mod_spec(k):
    return pl.BlockSpec((None, None, 1, D), lambda i, *_: (_mod_row(i), k, 0, 0))


def _norm_mod(x, g, sh, sc):
    ms = jnp.mean(x * x, axis=-1, keepdims=True)
    return (x * lax.rsqrt(ms + EPS) * g) * (1.0 + sc) + sh


def _head_rms(o, g):
    outs = []
    for h in range(HEADS):
        oh = o[:, h * HEAD_DIM:(h + 1) * HEAD_DIM]
        ms = jnp.mean(oh * oh, axis=-1, keepdims=True)
        outs.append(oh * lax.rsqrt(ms + EPS) * g)
    return jnp.concatenate(outs, axis=1)


HALO = 8
BF16_HALO = 16


def _is_seq_start(r):
    starts = (0, SEQ, LAT_ROWS, LAT_ROWS + CTX)
    return functools.reduce(jnp.logical_or, [r == s for s in starts])


def _zero_row(a, row, cond):
    s0 = row // 8 * 8
    sub = lax.broadcasted_iota(jnp.int32, (8, 1), 0)
    slab = jnp.where(jnp.logical_and(sub == row - s0, cond), 0.0, a[s0:s0 + 8])
    parts = [a[:s0]] * (s0 > 0) + [slab] + [a[s0 + 8:]] * (s0 + 8 < a.shape[0])
    return jnp.concatenate(parts, axis=0)


def _conv3(xh, w, r0):
    n = xh.shape[0]
    tm = n - 2 * HALO
    x_dn = pltpu.roll(xh, 1, 0)[HALO:HALO + tm]
    x_up = pltpu.roll(xh, n - 1, 0)[HALO:HALO + tm]
    for c in range(0, tm, CTX):
        nxt = r0 + c + CTX
        x_dn = _zero_row(x_dn, c, _is_seq_start(r0 + c))
        x_up = _zero_row(x_up, c + CTX - 1, jnp.logical_or(_is_seq_start(nxt), nxt == ROWS))
    return x_dn * w[0:1, :] + xh[HALO:HALO + tm] * w[1:2, :] + x_up * w[2:3, :]


def _halo_specs(tm, tc, col_of, rows):
    nb = ROWS // rows
    prev = pl.BlockSpec((rows, tc), lambda i, *a: (jnp.maximum(i * (tm // rows) - 1, 0), col_of(i, *a)))
    nxt = pl.BlockSpec((rows, tc), lambda i, *a: (jnp.minimum((i + 1) * (tm // rows), nb - 1), col_of(i, *a)))
    return prev, nxt


def _ada_kernel(s_ref, w_ref, b_ref, o_ref):
    s = s_ref[...]
    s = s * jax.nn.sigmoid(s)
    o_ref[...] = jnp.dot(s.astype(BF16), w_ref[...].astype(BF16),
                         preferred_element_type=F32) + b_ref[...]


def _ada(c, c_ctx, ada_w, ada_b):
    tn = 1536
    s = jnp.concatenate([c_ctx[None, :], c, jnp.zeros((5, D), F32)], axis=0)
    out = pl.pallas_call(
        _ada_kernel,
        grid=(DEPTH, N_ADA * D // tn),
        in_specs=[pl.BlockSpec((8, D), lambda l, j: (0, 0)),
                  pl.BlockSpec((None, D, tn), lambda l, j: (l, 0, j)),
                  pl.BlockSpec((None, 1, tn), lambda l, j: (l, 0, j))],
        out_specs=pl.BlockSpec((None, 8, tn), lambda l, j: (l, 0, j)),
        out_shape=jax.ShapeDtypeStruct((DEPTH, 8, N_ADA * D), F32),
        compiler_params=_params(("parallel", "parallel")),
        name="ada",
    )(s, ada_w, ada_b.reshape(DEPTH, 1, N_ADA * D))
    return out[:, :3].reshape(DEPTH, 3, N_ADA, 1, D)


def _proj_in_kernel(x_ref, g_ref, sh_ref, sc_ref, w_ref, o_ref, h_ref):
    @pl.when(pl.program_id(1) == 0)
    def _():
        h_ref[...] = _norm_mod(x_ref[...], g_ref[...], sh_ref[...], sc_ref[...]).astype(BF16)

    o_ref[...] = jnp.dot(h_ref[...], w_ref[...], preferred_element_type=F32).astype(BF16)


def _proj_in(x, g, mods, w, tn):
    n = w.shape[1]
    return pl.pallas_call(
        _proj_in_kernel,
        grid=(N_TILES, n // tn),
        in_specs=[pl.BlockSpec((TM, D), lambda i, j: (i, 0)),
                  pl.BlockSpec((1, D), lambda i, j: (0, 0)),
                  _mod_spec(0), _mod_spec(1),
                  pl.BlockSpec((D, tn), lambda i, j: (0, j))],
        out_specs=pl.BlockSpec((TM, tn), lambda i, j: (i, j)),
        out_shape=jax.ShapeDtypeStruct((ROWS, n), BF16),
        scratch_shapes=[pltpu.VMEM((TM, D), BF16)],
        compiler_params=_params(("parallel", "arbitrary")),
        name="proj_in",
    )(x, g.reshape(1, D), mods, mods, w)


def _mlp_kernel(x_ref, g_ref, sh_ref, sc_ref, gate_ref, w1_ref, w2_ref, fg_ref, o_ref,
                h_ref, acc_ref, *, final_norm):
    j = pl.program_id(1)

    @pl.when(j == 0)
    def _():
        h_ref[...] = _norm_mod(x_ref[...], g_ref[...], sh_ref[...], sc_ref[...]).astype(BF16)
        acc_ref[...] = jnp.zeros_like(acc_ref)

    a = jnp.dot(h_ref[...], w1_ref[...], preferred_element_type=F32)
    a = jnp.square(jnp.maximum(a, 0.0))
    acc_ref[...] += jnp.dot(a.astype(BF16), w2_ref[...], preferred_element_type=F32)

    @pl.when(j == pl.num_programs(1) - 1)
    def _():
        y = x_ref[...] + gate_ref[...] * acc_ref[...]
        if final_norm:
            ms = jnp.mean(y * y, axis=-1, keepdims=True)
            y = y * lax.rsqrt(ms + EPS) * fg_ref[...]
        o_ref[...] = y


def _mlp(x, g, mods, w1, w2, final_g, *, last):
    tf = 2048
    n_tiles = N_LAT_TILES if last else N_TILES
    return pl.pallas_call(
        functools.partial(_mlp_kernel, final_norm=last),
        grid=(n_tiles, D_FF // tf),
        in_specs=[pl.BlockSpec((TM, D), lambda i, j: (i, 0)),
                  pl.BlockSpec((1, D), lambda i, j: (0, 0)),
                  _mod_spec(3), _mod_spec(4), _mod_spec(5),
                  pl.BlockSpec((D, tf), lambda i, j: (0, j)),
                  pl.BlockSpec((tf, D), lambda i, j: (j, 0)),
                  pl.BlockSpec((1, D), lambda i, j: (0, 0))],
        out_specs=pl.BlockSpec((TM, D), lambda i, j: (i, 0)),
        out_shape=jax.ShapeDtypeStruct((n_tiles * TM, D), F32),
        scratch_shapes=[pltpu.VMEM((TM, D), BF16), pltpu.VMEM((TM, D), F32)],
        compiler_params=_params(("parallel", "arbitrary")),
        name="mlp",
    )(x, g.reshape(1, D), mods, mods, mods, w1, w2, final_g.reshape(1, D))


GDN_HG = 8
GDN_GROUPS = HEADS // GDN_HG
GDN_GATE_COL = 3
GDN_GL = 4 * GDN_HG
GDN_GW = GDN_GROUPS * HEAD_DIM
GDN_SLAB = 256


def _gdn_proj_kernel(x_ref, xp_ref, xn_ref, g_ref, sh_ref, sc_ref, w_ref, wab_ref, cw_ref,
                     alog_ref, dtb_ref, o_ref, gates_ref, gates_t_ref, h_ref):
    i, j = pl.program_id(0), pl.program_id(1)
    rows = slice(HALO, HALO + TM)

    @pl.when(j == 0)
    def _():
        x = jnp.concatenate([xp_ref[...], x_ref[...], xn_ref[...]], axis=0)
        h_ref[...] = _norm_mod(x, g_ref[...], sh_ref[...], sc_ref[...]).astype(BF16)
        ab = jnp.dot(h_ref[...], wab_ref[...], preferred_element_type=F32)[rows]
        lane = lax.broadcasted_iota(jnp.int32, ab.shape, 1) & 3
        t = ab + dtb_ref[...]
        softplus = jnp.maximum(t, 0.0) + jnp.log1p(jnp.exp(-jnp.abs(t)))
        gates = jnp.where(lane < 2, -jnp.exp(alog_ref[...]) * softplus, jax.nn.sigmoid(ab))
        gates_ref[...] = gates
        for hg in range(GDN_GROUPS):
            gt = gates[:, hg * HEAD_DIM:(hg + 1) * HEAD_DIM].T
            gates_t_ref[hg * GDN_GL:(hg + 1) * GDN_GL, :] = gt[0:GDN_GL, :]

    def columns(mode):
        scale = jnp.where(j == 0, HEAD_DIM ** -0.5, 1.0).astype(F32)
        for c0 in range(0, D, GDN_SLAB):
            cs = slice(c0, c0 + GDN_SLAB)
            z = jnp.dot(h_ref[...], w_ref[:, cs], preferred_element_type=F32)
            if mode == 'gate':
                o_ref[:, cs] = z[rows].astype(BF16)
                continue
            y = _conv3(z, cw_ref[:, cs], i * TM)
            y = y * jax.nn.sigmoid(y)
            if mode == 'v':
                o_ref[:, cs] = y.astype(BF16)
                continue
            for h0 in range(0, GDN_SLAB, HEAD_DIM):
                yh = y[:, h0:h0 + HEAD_DIM]
                ss = jnp.sum(yh * yh, axis=-1, keepdims=True)
                o_ref[:, c0 + h0:c0 + h0 + HEAD_DIM] = (yh * (lax.rsqrt(ss + EPS) * scale)).astype(BF16)

    pl.when(j < 2)(lambda: columns('qk'))
    pl.when(j == 2)(lambda: columns('v'))
    pl.when(j == 3)(lambda: columns('gate'))


def _gdn_proj(x, g, mods, w, wab, conv_w, alog_pad, dtb_pad):
    prev, nxt = _halo_specs(TM, D, lambda i, j: 0, HALO)
    full = lambda shape: pl.BlockSpec(shape, lambda i, j: (0, 0))
    return pl.pallas_call(
        _gdn_proj_kernel,
        grid=(N_TILES, 4),
        in_specs=[pl.BlockSpec((TM, D), lambda i, j: (i, 0)), prev, nxt,
                  full((1, D)), _mod_spec(0), _mod_spec(1),
                  pl.BlockSpec((D, D), lambda i, j: (0, j)),
                  full((D, GDN_GW)),
                  pl.BlockSpec((3, D), lambda i, j: (0, jnp.minimum(j, 2))),
                  full((1, GDN_GW)), full((1, GDN_GW))],
        out_specs=[pl.BlockSpec((TM, D), lambda i, j: (i, j)),
                   pl.BlockSpec((TM, GDN_GW), lambda i, j: (i, 0)),
                   pl.BlockSpec((GDN_GROUPS * GDN_GL, TM), lambda i, j: (0, i))],
        out_shape=[jax.ShapeDtypeStruct((ROWS, 4 * D), BF16),
                   jax.ShapeDtypeStruct((ROWS, GDN_GW), F32),
                   jax.ShapeDtypeStruct((GDN_GROUPS * GDN_GL, ROWS), F32)],
        scratch_shapes=[pltpu.VMEM((TM + 2 * HALO, D), BF16)],
        compiler_params=_params(("parallel", "arbitrary")),
        name="gdn_proj",
    )(x, x, x, g.reshape(1, D), mods, mods, w, wab, conv_w, alog_pad, dtb_pad)


GDN_TB = 256
GDN_NC = GDN_TB // CHUNK
GDN_STEPS = (CTX + SEQ) // GDN_TB
GDN_CTX_BLK = LAT_ROWS // GDN_TB
GDN_LAT_BLKS = SEQ // GDN_TB


def _split_dot(a_exact, x, *, x_is_lhs=False):
    hi = x.astype(BF16)
    r1 = x - hi.astype(F32)
    mid = r1.astype(BF16)
    lo = (r1 - mid.astype(F32)).astype(BF16)
    out = None
    for part in (hi, mid, lo):
        t = (jnp.dot(part, a_exact, preferred_element_type=F32) if x_is_lhs
             else jnp.dot(a_exact, part, preferred_element_type=F32))
        out = t if out is None else out + t
    return out


TRI_BASE_SHIFT = 3


def _tri_masks(r, c):
    same = lambda shift: (r >> shift) == (c >> shift)
    offs = [same(shift + 1) & jnp.logical_not(same(shift)) for shift in range(TRI_BASE_SHIFT, 6)]
    return same(TRI_BASE_SHIFT), offs


GDN_PACK = GDN_NC


def _block_diag(b16, lbm):
    zero = jnp.zeros_like(b16)
    return jnp.concatenate([jnp.where(m, b16, zero) for m in lbm], axis=0)


def _pdot(a, b, lbm):
    return jnp.dot(a.astype(BF16), _block_diag(b.astype(BF16), lbm), preferred_element_type=F32)


def _lane_blocks(lbm, parts):
    out = parts[-1]
    for m, part in zip(lbm[-2::-1], parts[-2::-1]):
        out = jnp.where(m, part, out)
    return out


def _tri_inverse_packed(lms, eye, masks, lbm):
    base, offs = masks
    lds = [jnp.where(base, lm, 0.0) for lm in lms]
    ps = [eye - ld for ld in lds]
    qs = [_pdot(ld, ld, lbm) for ld in lds]
    pqs = [_pdot(jnp.concatenate([q, p], axis=0), q, lbm) for q, p in zip(qs, ps)]
    ts = [p + pq[CHUNK:] for p, pq in zip(ps, pqs)]
    ts = [t + _pdot(t, pq[:CHUNK], lbm) for t, pq in zip(ts, pqs)]
    for off in offs:
        xs = [_pdot(jnp.where(off, lm, 0.0), t, lbm) for lm, t in zip(lms, ts)]
        ts = [t - _pdot(t, x, lbm) for t, x in zip(ts, xs)]
    return ts


def _gdn_scan_kernel(qf_ref, kf_ref, vf_ref, gf_ref, gtf_ref,
                     qb_ref, kb_ref, vb_ref, gb_ref, gtb_ref,
                     of_ref, ob_ref, s_ref):
    @pl.when(pl.program_id(1) == 0)
    def _():
        s_ref[...] = jnp.zeros_like(s_ref)

    r = lax.broadcasted_iota(jnp.int32, (CHUNK, GDN_TB), 0)
    lane = lax.broadcasted_iota(jnp.int32, (CHUNK, GDN_TB), 1)
    c = lane & (CHUNK - 1)
    lbm = [(lane >> 6) == i for i in range(GDN_PACK)]
    eye = (r == c).astype(F32)
    tri_masks = _tri_masks(r, c)
    r2 = lax.broadcasted_iota(jnp.int32, (GDN_TB, GDN_TB), 0)
    c2 = lax.broadcasted_iota(jnp.int32, (GDN_TB, GDN_TB), 1)
    same = (r2 >> 6) == (c2 >> 6)
    lower = jnp.where(same & (r2 >= c2), 1.0, 0.0).astype(BF16)
    upper = jnp.where(same & (r2 <= c2), 1.0, 0.0).astype(BF16)
    nt = (((1,), (1,)), ((), ()))
    tn = (((0,), (0,)), ((), ()))
    chunk = lambda a, ci: a[ci * CHUNK:(ci + 1) * CHUNK]
    zero_k = jnp.zeros((CHUNK, HEAD_DIM), BF16)

    dirs = ((qf_ref, kf_ref, vf_ref, gf_ref, gtf_ref, of_ref),
            (qb_ref, kb_ref, vb_ref, gb_ref, gtb_ref, ob_ref))
    streams = []
    for d, (q_ref, k_ref, v_ref, g_ref, gt_ref, o_ref) in enumerate(dirs):
        fwd = d == 0
        g_all = g_ref[...]
        gc_cols = _split_dot(lower if fwd else upper, g_all)
        gc_rows = _split_dot(upper if fwd else lower, gt_ref[...], x_is_lhs=True)
        last = CHUNK - 1 if fwd else 0
        for hl in range(GDN_HG):
            l0 = hl * HEAD_DIM
            gl = 4 * hl + d
            gcol = gc_cols[:, gl:gl + 1]
            glasts = [chunk(gcol, ci)[last:last + 1, :] for ci in range(GDN_PACK)]
            streams.append(dict(
                d=d, hl=hl, l0=l0, o_ref=o_ref,
                incl=(r >= c) if fwd else (r <= c),
                strict=(r > c) if fwd else (r < c),
                q=q_ref[:, l0:l0 + HEAD_DIM], k=k_ref[:, l0:l0 + HEAD_DIM], v=v_ref[:, l0:l0 + HEAD_DIM],
                gcol=gcol, glasts=glasts,
                beta=g_all[:, gl + 2:gl + 3],
                grow=gc_rows[gl:gl + 1, :]))

    for s in streams:
        gcol_p = _lane_blocks(lbm, [chunk(s['gcol'], ci) for ci in range(GDN_PACK)])
        s['decay'] = jnp.where(s['incl'], jnp.exp(jnp.where(s['incl'], gcol_p - s['grow'], 0.0)), 0.0)
        lhs = jnp.concatenate([jnp.concatenate([chunk(s['k'], ci), chunk(s['q'], ci)], axis=0)
                               for ci in range(GDN_PACK)], axis=1)
        rhs = jnp.concatenate([jnp.concatenate([chunk(s['k'], ci) if cj == ci else zero_k
                                                for cj in range(GDN_PACK)], axis=1)
                               for ci in range(GDN_PACK)], axis=0)
        s['kkqk'] = lax.dot_general(lhs, rhs, nt, preferred_element_type=F32)
    for s in streams:
        beta_p = _lane_blocks(lbm, [chunk(s['beta'], ci) for ci in range(GDN_PACK)])
        s['lm'] = jnp.where(s['strict'], beta_p * s['kkqk'][:CHUNK] * s['decay'], 0.0)
        s['attn'] = jnp.where(s['incl'], s['kkqk'][CHUNK:] * s['decay'], 0.0).astype(BF16)
        eg = jnp.exp(s['gcol'])
        glast_rows = jnp.concatenate([jnp.broadcast_to(g, (CHUNK, 1)) for g in s['glasts']], axis=0)
        q, k, v = (s[n].astype(F32) for n in 'qkv')
        s['qd'] = q * eg
        s['rhs'] = jnp.concatenate([v * s['beta'], k * (s['beta'] * eg)], axis=1).astype(BF16)
        s['kt'] = (k * jnp.exp(glast_rows - s['gcol'])).astype(BF16)
    tinvs = _tri_inverse_packed([s['lm'] for s in streams], eye, tri_masks, lbm)
    zero_t = jnp.zeros((CHUNK, GDN_TB), BF16)
    for s, tinv in zip(streams, tinvs):
        t16 = tinv.astype(BF16)
        s['uws'] = [jnp.dot(jnp.where(m, t16, zero_t), s['rhs'], preferred_element_type=F32).astype(BF16)
                    for m in lbm]
    for s in streams:
        uw_all = jnp.concatenate(s['uws'], axis=0)
        s['au_aw'] = [jnp.dot(jnp.where(m, s['attn'], zero_t), uw_all, preferred_element_type=F32)
                      for m in lbm]
    probs = []
    for s in streams:
        for ci in range(GDN_PACK):
            au_aw = s['au_aw'][ci]
            ku_kw = lax.dot_general(chunk(s['kt'], ci), s['uws'][ci], tn, preferred_element_type=F32)
            probs.append(dict(
                d=s['d'], hl=s['hl'], ci=ci, l0=s['l0'], o_ref=s['o_ref'],
                rows=slice(ci * CHUNK, (ci + 1) * CHUNK), glast=s['glasts'][ci],
                o0=au_aw[:, :HEAD_DIM], n=ku_kw[:, :HEAD_DIM],
                kq=jnp.concatenate([ku_kw[:, HEAD_DIM:], chunk(s['qd'], ci) - au_aw[:, HEAD_DIM:]],
                                   axis=0).astype(BF16)))

    states = [s_ref[i] for i in range(2 * GDN_HG)]
    for step in range(GDN_NC):
        for p in probs:
            if p['ci'] != (step if p['d'] == 0 else GDN_NC - 1 - step):
                continue
            si = p['d'] * GDN_HG + p['hl']
            ks_qs = jnp.dot(p['kq'], states[si].astype(BF16), preferred_element_type=F32)
            p['o_ref'][p['rows'], p['l0']:p['l0'] + HEAD_DIM] = (ks_qs[HEAD_DIM:] + p['o0']).astype(BF16)
            states[si] = states[si] * jnp.exp(p['glast']) - ks_qs[:HEAD_DIM] + p['n']
    for i in range(2 * GDN_HG):
        s_ref[i] = states[i]


def _gdn_scan(qkvg, gates, gates_t):
    ngroups = GDN_GROUPS
    hw = GDN_HG * HEAD_DIM

    def blk(p, t, bwd):
        b = p // ngroups
        lat = b * GDN_LAT_BLKS + (GDN_LAT_BLKS - t if bwd else t - 1)
        return jnp.where(t == 0, GDN_CTX_BLK + b, lat)

    def specs(bwd):
        col = lambda base: (lambda p, t: (blk(p, t, bwd), base * ngroups + p % ngroups))
        return [pl.BlockSpec((GDN_TB, hw), col(0)),
                pl.BlockSpec((GDN_TB, hw), col(1)),
                pl.BlockSpec((GDN_TB, hw), col(2)),
                pl.BlockSpec((GDN_TB, HEAD_DIM), lambda p, t: (blk(p, t, bwd), p % ngroups)),
                pl.BlockSpec((GDN_GL, GDN_TB), lambda p, t: (p % ngroups, blk(p, t, bwd)))]

    out_spec = lambda bwd: pl.BlockSpec((GDN_TB, hw), lambda p, t: (blk(p, t, bwd), p % ngroups))
    return pl.pallas_call(
        _gdn_scan_kernel,
        grid=(BATCH * ngroups, GDN_STEPS),
        in_specs=specs(False) + specs(True),
        out_specs=[out_spec(False), out_spec(True)],
        out_shape=[jax.ShapeDtypeStruct((ROWS, D), BF16)] * 2,
        scratch_shapes=[pltpu.VMEM((2 * GDN_HG, HEAD_DIM, HEAD_DIM), F32)],
        compiler_params=_params(("parallel", "arbitrary")),
        name="gdn_scan",
    )(qkvg, qkvg, qkvg, gates, gates_t, qkvg, qkvg, qkvg, gates, gates_t)


def _gdn_out_kernel(x_ref, of_ref, ob_ref, zg_ref, ng_ref, gate_ref, w_ref, o_ref):
    zg = zg_ref[...].astype(F32)
    o = of_ref[...].astype(F32) + ob_ref[...].astype(F32)
    y = _head_rms(o, ng_ref[...]) * (zg * jax.nn.sigmoid(zg))
    o_ref[...] = x_ref[...] + gate_ref[...] * jnp.dot(
        y.astype(BF16), w_ref[...], preferred_element_type=F32)


def _gdn_out(x, o_f, o_b, z, norm_g, mods, w_out):
    row = pl.BlockSpec((TM, D), lambda i: (i, 0))
    return pl.pallas_call(
        _gdn_out_kernel,
        grid=(N_TILES,),
        in_specs=[row, row, row,
                  pl.BlockSpec((TM, D), lambda i: (i, GDN_GATE_COL)),
                  pl.BlockSpec((1, HEAD_DIM), lambda i: (0, 0)),
                  _mod_spec(2),
                  pl.BlockSpec((D, D), lambda i: (0, 0))],
        out_specs=row,
        out_shape=jax.ShapeDtypeStruct((ROWS, D), F32),
        compiler_params=_params(("parallel",)),
        name="gdn_out",
    )(x, o_f, o_b, z, norm_g.reshape(1, HEAD_DIM), mods, w_out)


def _gdn_layer(x, mods, g1, w_in, conv_w, a_log, dt_bias, norm_g, w_out):
    ab = w_in[:, 4 * D:].reshape(D, 2, 2, HEADS)
    ab = jnp.transpose(ab, (0, 3, 1, 2)).reshape(D, GDN_GROUPS, GDN_GL)
    wab = jnp.pad(ab, ((0, 0), (0, 0), (0, HEAD_DIM - GDN_GL))).reshape(D, GDN_GW).astype(BF16)

    def pad_head_param(p):
        p = jnp.pad(jnp.transpose(p, (1, 0)), ((0, 0), (0, 2))).reshape(GDN_GROUPS, GDN_GL)
        return jnp.pad(p, ((0, 0), (0, HEAD_DIM - GDN_GL))).reshape(1, GDN_GW).astype(F32)

    qkvg, gates, gates_t = _gdn_proj(x, g1, mods, w_in[:, :4 * D].astype(BF16), wab, conv_w,
                                     pad_head_param(a_log), pad_head_param(dt_bias))
    o_f, o_b = _gdn_scan(qkvg, gates, gates_t)
    return _gdn_out(x, o_f, o_b, qkvg, norm_g, mods, w_out.astype(BF16))


def _sconv_out_kernel(x_ref, bg_ref, cg_ref, hv_ref, cgp_ref, hvp_ref, cgn_ref, hvn_ref,
                      cw_ref, gate_ref, w_ref, o_ref):
    f32 = lambda ref: ref[...].astype(F32)
    prev = (f32(cgp_ref) * f32(hvp_ref))[BF16_HALO - HALO:]
    nxt = (f32(cgn_ref) * f32(hvn_ref))[:HALO]
    xh = jnp.concatenate([prev, f32(cg_ref) * f32(hv_ref), nxt], axis=0)
    y = f32(bg_ref) * _conv3(xh, cw_ref[...], pl.program_id(0) * TM)
    o_ref[...] = x_ref[...] + gate_ref[...] * jnp.dot(
        y.astype(BF16), w_ref[...], preferred_element_type=F32)


def _sconv_layer(x, mods, g1, w_in, conv_w, w_out):
    z = _proj_in(x, g1, mods, w_in.astype(BF16), tn=1024)
    colspec = lambda cidx: pl.BlockSpec((TM, D), lambda i: (i, cidx))
    p1, n1 = _halo_specs(TM, D, lambda i: 1, BF16_HALO)
    p2, n2 = _halo_specs(TM, D, lambda i: 2, BF16_HALO)
    return pl.pallas_call(
        _sconv_out_kernel,
        grid=(N_TILES,),
        in_specs=[colspec(0), colspec(0), colspec(1), colspec(2), p1, p2, n1, n2,
                  pl.BlockSpec((3, D), lambda i: (0, 0)),
                  _mod_spec(2),
                  pl.BlockSpec((D, D), lambda i: (0, 0))],
        out_specs=colspec(0),
        out_shape=jax.ShapeDtypeStruct((ROWS, D), F32),
        compiler_params=_params(("parallel",)),
        name="sconv_out",
    )(x, z, z, z, z, z, z, z, conv_w, mods, w_out.astype(BF16))


def _rope_tables():
    t = jnp.arange(SEQ)
    row = (t // GRID_W).astype(F32)
    colp = (t % GRID_W).astype(F32)
    nf = DIFF_DH // 4
    inv = ROPE_BASE ** (-jnp.arange(nf, dtype=F32) / nf)
    ang = jnp.concatenate([row[:, None] * inv, colp[:, None] * inv], axis=-1)
    ang = jnp.tile(jnp.repeat(ang, 2, axis=-1), (1, HEAD_DIM // DIFF_DH))
    even = (jnp.arange(HEAD_DIM) % 2) == 0
    cos, sin = jnp.cos(ang), jnp.sin(ang)
    return cos, jnp.where(even, -sin, 0.0), jnp.where(even, 0.0, sin)


Q_SCALE = DIFF_DH ** -0.5 * math.log2(math.e)


DIFF_SLAB = 512


def _diff_proj_kernel(x_ref, g_ref, sh_ref, sc_ref, w_ref, wvt_ref, cos_ref, sa_ref, sb_ref,
                      qk_ref, vt_ref, h_ref):
    i, j = pl.program_id(0), pl.program_id(1)
    nt = (((1,), (1,)), ((), ()))

    @pl.when(j == 0)
    def _():
        h_ref[...] = _norm_mod(x_ref[...], g_ref[...], sh_ref[...], sc_ref[...]).astype(BF16)

    def columns(mode):
        scale = jnp.where(j == 0, Q_SCALE, 1.0).astype(F32)
        for c0 in range(0, D, DIFF_SLAB):
            if mode == 'v':
                vt_ref[c0:c0 + DIFF_SLAB, :] = lax.dot_general(
                    wvt_ref[c0:c0 + DIFF_SLAB, :], h_ref[...], nt, preferred_element_type=F32).astype(BF16)
                continue
            z = jnp.dot(h_ref[...], w_ref[:, c0:c0 + DIFF_SLAB], preferred_element_type=F32)
            for h0 in range(0, DIFF_SLAB, HEAD_DIM):
                x = z[:, h0:h0 + HEAD_DIM]
                if mode == 'rope':
                    x = (x * cos_ref[...] + pltpu.roll(x, HEAD_DIM - 1, 1) * sa_ref[...]
                         + pltpu.roll(x, 1, 1) * sb_ref[...])
                qk_ref[:, c0 + h0:c0 + h0 + HEAD_DIM] = (x * scale).astype(BF16)

    is_lat = i < N_LAT_TILES
    pl.when(jnp.logical_and(j < 2, is_lat))(lambda: columns('rope'))
    pl.when(jnp.logical_and(j < 2, jnp.logical_not(is_lat)))(lambda: columns('plain'))
    pl.when(j == 2)(lambda: columns('v'))


def _diff_proj(x, g, mods, w):
    tab = pl.BlockSpec((TM, HEAD_DIM), lambda i, j: (jnp.minimum(i, N_LAT_TILES - 1) % TILES_PER_BATCH, 0))
    return pl.pallas_call(
        _diff_proj_kernel,
        grid=(N_TILES, 3),
        in_specs=[pl.BlockSpec((TM, D), lambda i, j: (i, 0)),
                  pl.BlockSpec((1, D), lambda i, j: (0, 0)),
                  _mod_spec(0), _mod_spec(1),
                  pl.BlockSpec((D, D), lambda i, j: (0, jnp.minimum(j, 1))),
                  pl.BlockSpec((D, D), lambda i, j: (0, 0)),
                  tab, tab, tab],
        out_specs=[pl.BlockSpec((TM, D), lambda i, j: (i, jnp.minimum(j, 1))),
                   pl.BlockSpec((D, TM), lambda i, j: (0, i))],
        out_shape=[jax.ShapeDtypeStruct((ROWS, 2 * D), BF16),
                   jax.ShapeDtypeStruct((D, ROWS), BF16)],
        scratch_shapes=[pltpu.VMEM((TM, D), BF16)],
        compiler_params=_params(("parallel", "arbitrary")),
        name="diff_proj",
    )(x, g.reshape(1, D), mods, mods, w, w[:, 2 * D:].T, *_rope_tables())


def _lambda_full(lam_ref, lambda_init):
    lf = lam_ref[...]
    a = jnp.sum(lf[0:1, :] * lf[1:2, :], axis=-1, keepdims=True)
    b = jnp.sum(lf[2:3, :] * lf[3:4, :], axis=-1, keepdims=True)
    return jnp.exp(a) - jnp.exp(b) + lambda_init


ATT_TQ = 2048
ATT_TK = 1024
ATT_QW = 256


def _split_maps(q):
    lane = lax.broadcasted_iota(jnp.int32, q.shape, 1)
    zero = jnp.zeros_like(q)
    return jnp.where(lane < DIFF_DH, q, zero), jnp.where(lane >= DIFF_DH, q, zero)


ATT_ONES = 16


ATT_CHAINS = [(mi, slice(w * ATT_QW, (w + 1) * ATT_QW))
              for mi in range(2) for w in range(ATT_TQ // ATT_QW)]


def _attn_scores(k, qz_ref, mi, sl):
    nt = (((1,), (1,)), ((), ()))
    return lax.dot_general(k, qz_ref[mi, sl, :], nt, preferred_element_type=F32)


def _attn_consume(s, s_max, vt1, m_ref, acc_ref, mi, sl):
    m_old = m_ref[mi, :, sl]
    m_new = jnp.maximum(m_old, s_max)
    alpha = jnp.exp2(m_old - m_new)
    m_ref[mi, :, sl] = m_new
    p = jnp.exp2((s - m_new).astype(BF16))
    acc_ref[mi, :, sl] = alpha * acc_ref[mi, :, sl] + jnp.dot(vt1, p, preferred_element_type=F32)


def _with_ones(vt):
    return jnp.concatenate([vt, jnp.ones((ATT_ONES, vt.shape[1]), BF16)], axis=0)


def _attn_lat_kernel(lam_ref, q_ref, kc_ref, vct_ref, k_ref, vt_ref, o_ref,
                     qz_ref, m_ref, acc_ref, s0_ref, s1_ref, x0_ref, x1_ref, *, lambda_init, nk):
    j = pl.program_id(2)
    bufs = ((s0_ref, x0_ref), (s1_ref, x1_ref))

    def produce(buf, c, mi, sl):
        s = _attn_scores(k_ref[...], qz_ref, mi, sl)
        buf[0][c] = s
        buf[1][c] = jnp.max(s, axis=0, keepdims=True)

    @pl.when(j == 0)
    def _():
        q1, q2 = _split_maps(q_ref[...])
        qz_ref[0] = q1
        qz_ref[1] = q2
        m_ref[...] = jnp.full_like(m_ref, -jnp.inf)
        acc_ref[...] = jnp.zeros_like(acc_ref)
        for c, (mi, sl) in enumerate(ATT_CHAINS):
            produce(bufs[0], c, mi, sl)

    def step(cur, nxt):
        vt1 = _with_ones(vt_ref[...])
        vct1 = _with_ones(vct_ref[...])
        for c, (mi, sl) in enumerate(ATT_CHAINS):
            if nxt is not None:
                produce(nxt, c, mi, sl)
            _attn_consume(cur[0][c], cur[1][c], vt1, m_ref, acc_ref, mi, sl)
            if nxt is None:
                s = _attn_scores(kc_ref[...], qz_ref, mi, sl)
                _attn_consume(s, jnp.max(s, axis=0, keepdims=True), vct1, m_ref, acc_ref, mi, sl)

    middle = jnp.logical_and(j > 0, j < nk)
    odd = lax.rem(j, 2) == 1
    pl.when(jnp.logical_and(middle, odd))(lambda: step(bufs[0], bufs[1]))
    pl.when(jnp.logical_and(middle, jnp.logical_not(odd)))(lambda: step(bufs[1], bufs[0]))

    @pl.when(j == nk)
    def _():
        step(bufs[(nk - 1) % 2], None)
        lmb = _lambda_full(lam_ref, lambda_init)
        num = lambda mi: acc_ref[mi, :HEAD_DIM, :]
        den = lambda mi: acc_ref[mi, HEAD_DIM:HEAD_DIM + 1, :]
        ot = num(0) / den(0) - lmb * (num(1) / den(1))
        o_ref[...] = ot.T


def _attn_ctx_kernel(lam_ref, q_ref, k_ref, vt_ref, o_ref, *, lambda_init):
    nt = (((1,), (1,)), ((), ()))
    outs = []
    for qz in _split_maps(q_ref[...]):
        s = lax.dot_general(k_ref[...], qz, nt, preferred_element_type=F32)
        p = jnp.exp2(s - jnp.max(s, axis=0, keepdims=True))
        o = jnp.dot(vt_ref[...], p.astype(BF16), preferred_element_type=F32)
        outs.append(o / jnp.sum(p, axis=0, keepdims=True))
    o_ref[...] = (outs[0] - _lambda_full(lam_ref, lambda_init) * outs[1]).T


def _diff_attention(qk, vt, lam, lambda_init):
    nq, nk = SEQ // ATT_TQ, SEQ // ATT_TK
    ctx_blk = LAT_ROWS // CTX
    lam_spec3 = pl.BlockSpec((4, DIFF_DH), lambda p, i, j: (0, 0))
    o_lat = pl.pallas_call(
        functools.partial(_attn_lat_kernel, lambda_init=lambda_init, nk=nk),
        grid=(BATCH * HEADS, nq, nk + 1),
        in_specs=[lam_spec3,
                  pl.BlockSpec((ATT_TQ, HEAD_DIM), lambda p, i, j: ((p // HEADS) * nq + i, p % HEADS)),
                  pl.BlockSpec((CTX, HEAD_DIM), lambda p, i, j: (ctx_blk + p // HEADS, HEADS + p % HEADS)),
                  pl.BlockSpec((HEAD_DIM, CTX), lambda p, i, j: (p % HEADS, ctx_blk + p // HEADS)),
                  pl.BlockSpec((ATT_TK, HEAD_DIM),
                               lambda p, i, j: ((p // HEADS) * nk + jnp.minimum(j, nk - 1), HEADS + p % HEADS)),
                  pl.BlockSpec((HEAD_DIM, ATT_TK),
                               lambda p, i, j: (p % HEADS, (p // HEADS) * nk + jnp.maximum(j - 1, 0)))],
        out_specs=pl.BlockSpec((ATT_TQ, HEAD_DIM), lambda p, i, j: ((p // HEADS) * nq + i, p % HEADS)),
        out_shape=jax.ShapeDtypeStruct((LAT_ROWS, D), F32),
        scratch_shapes=[pltpu.VMEM((2, ATT_TQ, HEAD_DIM), BF16),
                        pltpu.VMEM((2, 1, ATT_TQ), F32),
                        pltpu.VMEM((2, HEAD_DIM + ATT_ONES, ATT_TQ), F32),
                        pltpu.VMEM((len(ATT_CHAINS), ATT_TK, ATT_QW), F32),
                        pltpu.VMEM((len(ATT_CHAINS), ATT_TK, ATT_QW), F32),
                        pltpu.VMEM((len(ATT_CHAINS), 1, ATT_QW), F32),
                        pltpu.VMEM((len(ATT_CHAINS), 1, ATT_QW), F32)],
        compiler_params=_params(("parallel", "parallel", "arbitrary")),
        name="diff_attn_lat",
    )(lam, qk, qk, vt, qk, vt)
    o_ctx = pl.pallas_call(
        functools.partial(_attn_ctx_kernel, lambda_init=lambda_init),
        grid=(BATCH, HEADS),
        in_specs=[pl.BlockSpec((4, DIFF_DH), lambda b, h: (0, 0)),
                  pl.BlockSpec((CTX, HEAD_DIM), lambda b, h: (ctx_blk + b, h)),
                  pl.BlockSpec((CTX, HEAD_DIM), lambda b, h: (ctx_blk + b, HEADS + h)),
                  pl.BlockSpec((HEAD_DIM, CTX), lambda b, h: (h, ctx_blk + b))],
        out_specs=pl.BlockSpec((CTX, HEAD_DIM), lambda b, h: (b, h)),
        out_shape=jax.ShapeDtypeStruct((CTX_ROWS, D), F32),
        compiler_params=_params(("parallel", "parallel")),
        name="diff_attn_ctx",
    )(lam, qk, qk, vt)
    return o_lat, o_ctx


def _diff_out_kernel(x_ref, ol_ref, oc_ref, ng_ref, gate_ref, w_ref, o_ref, *, out_scale):
    is_ctx = pl.program_id(0) >= N_LAT_TILES
    o = jnp.where(is_ctx, oc_ref[...], ol_ref[...])
    y = _head_rms(o, ng_ref[...]) * out_scale
    o_ref[...] = x_ref[...] + gate_ref[...] * jnp.dot(
        y.astype(BF16), w_ref[...], preferred_element_type=F32)


def _diff_layer(x, mods, g1, w_in, lam, norm_g, w_out, lambda_init):
    qk, vt = _diff_proj(x, g1, mods, w_in.astype(BF16))
    o_lat, o_ctx = _diff_attention(qk, vt, lam, lambda_init)
    row = pl.BlockSpec((TM, D), lambda i: (i, 0))
    return pl.pallas_call(
        functools.partial(_diff_out_kernel, out_scale=1.0 - lambda_init),
        grid=(N_TILES,),
        in_specs=[row,
                  pl.BlockSpec((TM, D), lambda i: (jnp.minimum(i, N_LAT_TILES - 1), 0)),
                  pl.BlockSpec((TM, D), lambda i: (0, 0)),
                  pl.BlockSpec((1, HEAD_DIM), lambda i: (0, 0)),
                  _mod_spec(2),
                  pl.BlockSpec((D, D), lambda i: (0, 0))],
        out_specs=row,
        out_shape=jax.ShapeDtypeStruct((ROWS, D), F32),
        compiler_params=_params(("parallel",)),
        name="diff_out",
    )(x, o_lat, o_ctx, norm_g.reshape(1, HEAD_DIM), mods, w_out.astype(BF16))


def kernel(x, c, ctx, c_ctx, norm1_g, norm2_g, ada_w, ada_b, mlp_w1, mlp_w2, gdn_w_in, gdn_conv, gdn_a_log, gdn_dt_bias, gdn_norm_g, gdn_w_out, sconv_w_in, sconv_conv, sconv_w_out, diff_w_in, diff_lambda, diff_norm_g, diff_w_out, final_g):
    mods = _ada(c, c_ctx, ada_w, ada_b)
    h = jnp.concatenate([x.reshape(LAT_ROWS, D), ctx.reshape(CTX_ROWS, D)], axis=0)
    for i in range(DEPTH):
        kind, j = i % 3, i // 3
        if kind == 0:
            h = _gdn_layer(h, mods[i], norm1_g[i], gdn_w_in[j], gdn_conv[j], gdn_a_log[j],
                           gdn_dt_bias[j], gdn_norm_g[j], gdn_w_out[j])
        elif kind == 1:
            h = _sconv_layer(h, mods[i], norm1_g[i], sconv_w_in[j], sconv_conv[j], sconv_w_out[j])
        else:
            lambda_init = 0.8 - 0.6 * math.exp(-0.3 * i)
            h = _diff_layer(h, mods[i], norm1_g[i], diff_w_in[j], diff_lambda[j],
                            diff_norm_g[j], diff_w_out[j], lambda_init)
        h = _mlp(h, norm2_g[i], mods[i], mlp_w1[i].astype(BF16), mlp_w2[i].astype(BF16),
                 final_g, last=i == DEPTH - 1)
    return h.reshape(BATCH, SEQ, D)
```

```python
import functools
import math

import jax
import jax.numpy as jnp
from jax import lax
from jax.experimental import pallas as pl
from jax.experimental.pallas import tpu as pltpu

F32 = jnp.float32
BF16 = jnp.bfloat16

D = 1024
BATCH = 2
SEQ = 8192
CTX = 256
DEPTH = 4
N_ADA = 6
D_FF = 4 * D
EPS = 1e-6
HEADS = 8
HEAD_DIM = 128
CHUNK = 64
DIFF_DH = 64
GRID_W = 64
ROPE_BASE = 10000.0

LAT_ROWS = BATCH * SEQ
CTX_ROWS = BATCH * CTX
ROWS = LAT_ROWS + CTX_ROWS
TM = 512
N_TILES = ROWS // TM
N_LAT_TILES = LAT_ROWS // TM
TILES_PER_BATCH = SEQ // TM
VMEM_LIMIT = 56 * 1024 * 1024


def _params(sem, vmem=VMEM_LIMIT):
    return pltpu.CompilerParams(dimension_semantics=sem, vmem_limit_bytes=vmem)


def _mod_row(i):
    return jnp.where(i >= N_LAT_TILES, 0, 1 + i // TILES_PER_BATCH)


def _mod_spec(k):
    return pl.BlockSpec((None, None, 1, D), lambda i, *_: (_mod_row(i), k, 0, 0))


def _norm_mod(x, g, sh, sc):
    ms = jnp.mean(x * x, axis=-1, keepdims=True)
    return (x * lax.rsqrt(ms + EPS) * g) * (1.0 + sc) + sh


def _head_rms(o, g):
    outs = []
    for h in range(HEADS):
        oh = o[:, h * HEAD_DIM:(h + 1) * HEAD_DIM]
        ms = jnp.mean(oh * oh, axis=-1, keepdims=True)
        outs.append(oh * lax.rsqrt(ms + EPS) * g)
    return jnp.concatenate(outs, axis=1)


HALO = 8
BF16_HALO = 16


def _is_seq_start(r):
    starts = (0, SEQ, LAT_ROWS, LAT_ROWS + CTX)
    return functools.reduce(jnp.logical_or, [r == s for s in starts])


def _zero_row(a, row, cond):
    s0 = row // 8 * 8
    sub = lax.broadcasted_iota(jnp.int32, (8, 1), 0)
    slab = jnp.where(jnp.logical_and(sub == row - s0, cond), 0.0, a[s0:s0 + 8])
    parts = [a[:s0]] * (s0 > 0) + [slab] + [a[s0 + 8:]] * (s0 + 8 < a.shape[0])
    return jnp.concatenate(parts, axis=0)


def _conv3(xh, w, r0):
    n = xh.shape[0]
    tm = n - 2 * HALO
    x_dn = pltpu.roll(xh, 1, 0)[HALO:HALO + tm]
    x_up = pltpu.roll(xh, n - 1, 0)[HALO:HALO + tm]
    for c in range(0, tm, CTX):
        nxt = r0 + c + CTX
        x_dn = _zero_row(x_dn, c, _is_seq_start(r0 + c))
        x_up = _zero_row(x_up, c + CTX - 1, jnp.logical_or(_is_seq_start(nxt), nxt == ROWS))
    return x_dn * w[0:1, :] + xh[HALO:HALO + tm] * w[1:2, :] + x_up * w[2:3, :]


def _halo_specs(tm, tc, col_of, rows):
    nb = ROWS // rows
    prev = pl.BlockSpec((rows, tc), lambda i, *a: (jnp.maximum(i * (tm // rows) - 1, 0), col_of(i, *a)))
    nxt = pl.BlockSpec((rows, tc), lambda i, *a: (jnp.minimum((i + 1) * (tm // rows), nb - 1), col_of(i, *a)))
    return prev, nxt


def _ada_kernel(s_ref, w_ref, b_ref, o_ref):
    s = s_ref[...]
    s = s * jax.nn.sigmoid(s)
    o_ref[...] = jnp.dot(s.astype(BF16), w_ref[...].astype(BF16),
                         preferred_element_type=F32) + b_ref[...]


def _ada(c, c_ctx, ada_w, ada_b):
    tn = 1536
    s = jnp.concatenate([c_ctx[None, :], c, jnp.zeros((5, D), F32)], axis=0)
    out = pl.pallas_call(
        _ada_kernel,
        grid=(DEPTH, N_ADA * D // tn),
        in_specs=[pl.BlockSpec((8, D), lambda l, j: (0, 0)),
                  pl.BlockSpec((None, D, tn), lambda l, j: (l, 0, j)),
                  pl.BlockSpec((None, 1, tn), lambda l, j: (l, 0, j))],
        out_specs=pl.BlockSpec((None, 8, tn), lambda l, j: (l, 0, j)),
        out_shape=jax.ShapeDtypeStruct((DEPTH, 8, N_ADA * D), F32),
        compiler_params=_params(("parallel", "parallel")),
        name="ada",
    )(s, ada_w, ada_b.reshape(DEPTH, 1, N_ADA * D))
    return out[:, :3].reshape(DEPTH, 3, N_ADA, 1, D)


def _proj_in_kernel(x_ref, g_ref, sh_ref, sc_ref, w_ref, o_ref, h_ref):
    @pl.when(pl.program_id(1) == 0)
    def _():
        h_ref[...] = _norm_mod(x_ref[...], g_ref[...], sh_ref[...], sc_ref[...]).astype(BF16)

    o_ref[...] = jnp.dot(h_ref[...], w_ref[...], preferred_element_type=F32).astype(BF16)


def _proj_in(x, g, mods, w, tn):
    n = w.shape[1]
    return pl.pallas_call(
        _proj_in_kernel,
        grid=(N_TILES, n // tn),
        in_specs=[pl.BlockSpec((TM, D), lambda i, j: (i, 0)),
                  pl.BlockSpec((1, D), lambda i, j: (0, 0)),
                  _mod_spec(0), _mod_spec(1),
                  pl.BlockSpec((D, tn), lambda i, j: (0, j))],
        out_specs=pl.BlockSpec((TM, tn), lambda i, j: (i, j)),
        out_shape=jax.ShapeDtypeStruct((ROWS, n), BF16),
        scratch_shapes=[pltpu.VMEM((TM, D), BF16)],
        compiler_params=_params(("parallel", "arbitrary")),
        name="proj_in",
    )(x, g.reshape(1, D), mods, mods, w)


def _mlp_kernel(x_ref, g_ref, sh_ref, sc_ref, gate_ref, w1_ref, w2_ref, fg_ref, o_ref,
                h_ref, acc_ref, *, final_norm):
    j = pl.program_id(1)

    @pl.when(j == 0)
    def _():
        h_ref[...] = _norm_mod(x_ref[...], g_ref[...], sh_ref[...], sc_ref[...]).astype(BF16)
        acc_ref[...] = jnp.zeros_like(acc_ref)

    a = jnp.dot(h_ref[...], w1_ref[...], preferred_element_type=F32)
    a = jnp.square(jnp.maximum(a, 0.0))
    acc_ref[...] += jnp.dot(a.astype(BF16), w2_ref[...], preferred_element_type=F32)

    @pl.when(j == pl.num_programs(1) - 1)
    def _():
        y = x_ref[...] + gate_ref[...] * acc_ref[...]
        if final_norm:
            ms = jnp.mean(y * y, axis=-1, keepdims=True)
            y = y * lax.rsqrt(ms + EPS) * fg_ref[...]
        o_ref[...] = y


def _mlp(x, g, mods, w1, w2, final_g, *, last):
    tf = D_FF
    n_tiles = N_LAT_TILES if last else N_TILES
    return pl.pallas_call(
        functools.partial(_mlp_kernel, final_norm=last),
        grid=(n_tiles, D_FF // tf),
        in_specs=[pl.BlockSpec((TM, D), lambda i, j: (i, 0)),
                  pl.BlockSpec((1, D), lambda i, j: (0, 0)),
                  _mod_spec(3), _mod_spec(4), _mod_spec(5),
                  pl.BlockSpec((D, tf), lambda i, j: (0, j)),
                  pl.BlockSpec((tf, D), lambda i, j: (j, 0)),
                  pl.BlockSpec((1, D), lambda i, j: (0, 0))],
        out_specs=pl.BlockSpec((TM, D), lambda i, j: (i, 0)),
        out_shape=jax.ShapeDtypeStruct((n_tiles * TM, D), F32),
        scratch_shapes=[pltpu.VMEM((TM, D), BF16), pltpu.VMEM((TM, D), F32)],
        compiler_params=_params(("parallel", "arbitrary")),
        name="mlp",
    )(x, g.reshape(1, D), mods, mods, mods, w1, w2, final_g.reshape(1, D))


GDN_HG = 8
GDN_GROUPS = HEADS // GDN_HG
GDN_GATE_COL = 3
GDN_GL = 4 * GDN_HG
GDN_GW = GDN_GROUPS * HEAD_DIM
GDN_SLAB = 256


def _gdn_proj_kernel(x_ref, xp_ref, xn_ref, g_ref, sh_ref, sc_ref, w_ref, wab_ref, cw_ref,
                     alog_ref, dtb_ref, o_ref, gates_ref, gates_t_ref, h_ref):
    i, j = pl.program_id(0), pl.program_id(1)
    rows = slice(HALO, HALO + TM)

    @pl.when(j == 0)
    def _():
        x = jnp.concatenate([xp_ref[...], x_ref[...], xn_ref[...]], axis=0)
        h_ref[...] = _norm_mod(x, g_ref[...], sh_ref[...], sc_ref[...]).astype(BF16)
        ab = jnp.dot(h_ref[...], wab_ref[...], preferred_element_type=F32)[rows]
        lane = lax.broadcasted_iota(jnp.int32, ab.shape, 1) & 3
        t = ab + dtb_ref[...]
        softplus = jnp.maximum(t, 0.0) + jnp.log1p(jnp.exp(-jnp.abs(t)))
        gates = jnp.where(lane < 2, -jnp.exp(alog_ref[...]) * softplus, jax.nn.sigmoid(ab))
        gates_ref[...] = gates
        for hg in range(GDN_GROUPS):
            gt = gates[:, hg * HEAD_DIM:(hg + 1) * HEAD_DIM].T
            gates_t_ref[hg * GDN_GL:(hg + 1) * GDN_GL, :] = gt[0:GDN_GL, :]

    def columns(mode):
        scale = jnp.where(j == 0, HEAD_DIM ** -0.5, 1.0).astype(F32)
        for c0 in range(0, D, GDN_SLAB):
            cs = slice(c0, c0 + GDN_SLAB)
            z = jnp.dot(h_ref[...], w_ref[:, cs], preferred_element_type=F32)
            if mode == 'gate':
                o_ref[:, cs] = z[rows].astype(BF16)
                continue
            y = _conv3(z, cw_ref[:, cs], i * TM)
            y = y * jax.nn.sigmoid(y)
            if mode == 'v':
                o_ref[:, cs] = y.astype(BF16)
                continue
            for h0 in range(0, GDN_SLAB, HEAD_DIM):
                yh = y[:, h0:h0 + HEAD_DIM]
                ss = jnp.sum(yh * yh, axis=-1, keepdims=True)
                o_ref[:, c0 + h0:c0 + h0 + HEAD_DIM] = (yh * (lax.rsqrt(ss + EPS) * scale)).astype(BF16)

    pl.when(j < 2)(lambda: columns('qk'))
    pl.when(j == 2)(lambda: columns('v'))
    pl.when(j == 3)(lambda: columns('gate'))


def _gdn_proj(x, g, mods, w, wab, conv_w, alog_pad, dtb_pad):
    prev, nxt = _halo_specs(TM, D, lambda i, j: 0, HALO)
    full = lambda shape: pl.BlockSpec(shape, lambda i, j: (0, 0))
    return pl.pallas_call(
        _gdn_proj_kernel,
        grid=(N_TILES, 4),
        in_specs=[pl.BlockSpec((TM, D), lambda i, j: (i, 0)), prev, nxt,
                  full((1, D)), _mod_spec(0), _mod_spec(1),
                  pl.BlockSpec((D, D), lambda i, j: (0, j)),
                  full((D, GDN_GW)),
                  pl.BlockSpec((3, D), lambda i, j: (0, jnp.minimum(j, 2))),
                  full((1, GDN_GW)), full((1, GDN_GW))],
        out_specs=[pl.BlockSpec((TM, D), lambda i, j: (i, j)),
                   pl.BlockSpec((TM, GDN_GW), lambda i, j: (i, 0)),
                   pl.BlockSpec((GDN_GROUPS * GDN_GL, TM), lambda i, j: (0, i))],
        out_shape=[jax.ShapeDtypeStruct((ROWS, 4 * D), BF16),
                   jax.ShapeDtypeStruct((ROWS, GDN_GW), F32),
                   jax.ShapeDtypeStruct((GDN_GROUPS * GDN_GL, ROWS), F32)],
        scratch_shapes=[pltpu.VMEM((TM + 2 * HALO, D), BF16)],
        compiler_params=_params(("parallel", "arbitrary")),
        name="gdn_proj",
    )(x, x, x, g.reshape(1, D), mods, mods, w, wab, conv_w, alog_pad, dtb_pad)


GDN_TB = 256
GDN_NC = GDN_TB // CHUNK
GDN_STEPS = (CTX + SEQ) // GDN_TB
GDN_CTX_BLK = LAT_ROWS // GDN_TB
GDN_LAT_BLKS = SEQ // GDN_TB


def _split_dot(a_exact, x, *, x_is_lhs=False):
    hi = x.astype(BF16)
    r1 = x - hi.astype(F32)
    mid = r1.astype(BF16)
    lo = (r1 - mid.astype(F32)).astype(BF16)
    out = None
    for part in (hi, mid, lo):
        t = (jnp.dot(part, a_exact, preferred_element_type=F32) if x_is_lhs
             else jnp.dot(a_exact, part, preferred_element_type=F32))
        out = t if out is None else out + t
    return out


TRI_BASE_SHIFT = 3


def _tri_masks(r, c):
    same = lambda shift: (r >> shift) == (c >> shift)
    offs = [same(shift + 1) & jnp.logical_not(same(shift)) for shift in range(TRI_BASE_SHIFT, 6)]
    return same(TRI_BASE_SHIFT), offs


GDN_PACK = GDN_NC


def _block_diag(b16, lbm):
    zero = jnp.zeros_like(b16)
    return jnp.concatenate([jnp.where(m, b16, zero) for m in lbm], axis=0)


def _pdot(a, b, lbm):
    return jnp.dot(a.astype(BF16), _block_diag(b.astype(BF16), lbm), preferred_element_type=F32)


def _lane_blocks(lbm, parts):
    out = parts[-1]
    for m, part in zip(lbm[-2::-1], parts[-2::-1]):
        out = jnp.where(m, part, out)
    return out


def _tri_inverse_packed(lms, eye, masks, lbm):
    base, offs = masks
    lds = [jnp.where(base, lm, 0.0) for lm in lms]
    ps = [eye - ld for ld in lds]
    qs = [_pdot(ld, ld, lbm) for ld in lds]
    pqs = [_pdot(jnp.concatenate([q, p], axis=0), q, lbm) for q, p in zip(qs, ps)]
    ts = [p + pq[CHUNK:] for p, pq in zip(ps, pqs)]
    ts = [t + _pdot(t, pq[:CHUNK], lbm) for t, pq in zip(ts, pqs)]
    for off in offs:
        xs = [_pdot(jnp.where(off, lm, 0.0), t, lbm) for lm, t in zip(lms, ts)]
        ts = [t - _pdot(t, x, lbm) for t, x in zip(ts, xs)]
    return ts


def _gdn_scan_kernel(qf_ref, kf_ref, vf_ref, gf_ref, gtf_ref,
                     qb_ref, kb_ref, vb_ref, gb_ref, gtb_ref,
                     of_ref, ob_ref, s_ref):
    @pl.when(pl.program_id(1) == 0)
    def _():
        s_ref[...] = jnp.zeros_like(s_ref)

    r = lax.broadcasted_iota(jnp.int32, (CHUNK, GDN_TB), 0)
    lane = lax.broadcasted_iota(jnp.int32, (CHUNK, GDN_TB), 1)
    c = lane & (CHUNK - 1)
    lbm = [(lane >> 6) == i for i in range(GDN_PACK)]
    eye = (r == c).astype(F32)
    tri_masks = _tri_masks(r, c)
    r2 = lax.broadcasted_iota(jnp.int32, (GDN_TB, GDN_TB), 0)
    c2 = lax.broadcasted_iota(jnp.int32, (GDN_TB, GDN_TB), 1)
    same = (r2 >> 6) == (c2 >> 6)
    lower = jnp.where(same & (r2 >= c2), 1.0, 0.0).astype(BF16)
    upper = jnp.where(same & (r2 <= c2), 1.0, 0.0).astype(BF16)
    nt = (((1,), (1,)), ((), ()))
    tn = (((0,), (0,)), ((), ()))
    chunk = lambda a, ci: a[ci * CHUNK:(ci + 1) * CHUNK]
    zero_k = jnp.zeros((CHUNK, HEAD_DIM), BF16)

    dirs = ((qf_ref, kf_ref, vf_ref, gf_ref, gtf_ref, of_ref),
            (qb_ref, kb_ref, vb_ref, gb_ref, gtb_ref, ob_ref))
    streams = []
    for d, (q_ref, k_ref, v_ref, g_ref, gt_ref, o_ref) in enumerate(dirs):
        fwd = d == 0
        g_all = g_ref[...]
        gc_cols = _split_dot(lower if fwd else upper, g_all)
        gc_rows = _split_dot(upper if fwd else lower, gt_ref[...], x_is_lhs=True)
        last = CHUNK - 1 if fwd else 0
        for hl in range(GDN_HG):
            l0 = hl * HEAD_DIM
            gl = 4 * hl + d
            gcol = gc_cols[:, gl:gl + 1]
            glasts = [chunk(gcol, ci)[last:last + 1, :] for ci in range(GDN_PACK)]
            streams.append(dict(
                d=d, hl=hl, l0=l0, o_ref=o_ref,
                incl=(r >= c) if fwd else (r <= c),
                strict=(r > c) if fwd else (r < c),
                q=q_ref[:, l0:l0 + HEAD_DIM], k=k_ref[:, l0:l0 + HEAD_DIM], v=v_ref[:, l0:l0 + HEAD_DIM],
                gcol=gcol, glasts=glasts,
                beta=g_all[:, gl + 2:gl + 3],
                grow=gc_rows[gl:gl + 1, :]))

    for s in streams:
        gcol_p = _lane_blocks(lbm, [chunk(s['gcol'], ci) for ci in range(GDN_PACK)])
        s['decay'] = jnp.where(s['incl'], jnp.exp(jnp.where(s['incl'], gcol_p - s['grow'], 0.0)), 0.0)
        lhs = jnp.concatenate([jnp.concatenate([chunk(s['k'], ci), chunk(s['q'], ci)], axis=0)
                               for ci in range(GDN_PACK)], axis=1)
        rhs = jnp.concatenate([jnp.concatenate([chunk(s['k'], ci) if cj == ci else zero_k
                                                for cj in range(GDN_PACK)], axis=1)
                               for ci in range(GDN_PACK)], axis=0)
        s['kkqk'] = lax.dot_general(lhs, rhs, nt, preferred_element_type=F32)
    for s in streams:
        beta_p = _lane_blocks(lbm, [chunk(s['beta'], ci) for ci in range(GDN_PACK)])
        s['lm'] = jnp.where(s['strict'], beta_p * s['kkqk'][:CHUNK] * s['decay'], 0.0)
        s['attn'] = jnp.where(s['incl'], s['kkqk'][CHUNK:] * s['decay'], 0.0).astype(BF16)
        eg = jnp.exp(s['gcol'])
        glast_rows = jnp.concatenate([jnp.broadcast_to(g, (CHUNK, 1)) for g in s['glasts']], axis=0)
        q, k, v = (s[n].astype(F32) for n in 'qkv')
        s['qd'] = q * eg
        s['rhs'] = jnp.concatenate([v * s['beta'], k * (s['beta'] * eg)], axis=1).astype(BF16)
        s['kt'] = (k * jnp.exp(glast_rows - s['gcol'])).astype(BF16)
    tinvs = _tri_inverse_packed([s['lm'] for s in streams], eye, tri_masks, lbm)
    zero_t = jnp.zeros((CHUNK, GDN_TB), BF16)
    for s, tinv in zip(streams, tinvs):
        t16 = tinv.astype(BF16)
        s['uws'] = [jnp.dot(jnp.where(m, t16, zero_t), s['rhs'], preferred_element_type=F32).astype(BF16)
                    for m in lbm]
    for s in streams:
        uw_all = jnp.concatenate(s['uws'], axis=0)
        s['au_aw'] = [jnp.dot(jnp.where(m, s['attn'], zero_t), uw_all, preferred_element_type=F32)
                      for m in lbm]
    probs = []
    for s in streams:
        for ci in range(GDN_PACK):
            au_aw = s['au_aw'][ci]
            ku_kw = lax.dot_general(chunk(s['kt'], ci), s['uws'][ci], tn, preferred_element_type=F32)
            probs.append(dict(
                d=s['d'], hl=s['hl'], ci=ci, l0=s['l0'], o_ref=s['o_ref'],
                rows=slice(ci * CHUNK, (ci + 1) * CHUNK), glast=s['glasts'][ci],
                o0=au_aw[:, :HEAD_DIM], n=ku_kw[:, :HEAD_DIM],
                kq=jnp.concatenate([ku_kw[:, HEAD_DIM:], chunk(s['qd'], ci) - au_aw[:, HEAD_DIM:]],
                                   axis=0).astype(BF16)))

    states = [s_ref[i] for i in range(2 * GDN_HG)]
    for step in range(GDN_NC):
        for p in probs:
            if p['ci'] != (step if p['d'] == 0 else GDN_NC - 1 - step):
                continue
            si = p['d'] * GDN_HG + p['hl']
            ks_qs = jnp.dot(p['kq'], states[si].astype(BF16), preferred_element_type=F32)
            p['o_ref'][p['rows'], p['l0']:p['l0'] + HEAD_DIM] = (ks_qs[HEAD_DIM:] + p['o0']).astype(BF16)
            states[si] = states[si] * jnp.exp(p['glast']) - ks_qs[:HEAD_DIM] + p['n']
    for i in range(2 * GDN_HG):
        s_ref[i] = states[i]


def _gdn_scan(qkvg, gates, gates_t):
    ngroups = GDN_GROUPS
    hw = GDN_HG * HEAD_DIM

    def blk(p, t, bwd):
        b = p // ngroups
        lat = b * GDN_LAT_BLKS + (GDN_LAT_BLKS - t if bwd else t - 1)
        return jnp.where(t == 0, GDN_CTX_BLK + b, lat)

    def specs(bwd):
        col = lambda base: (lambda p, t: (blk(p, t, bwd), base * ngroups + p % ngroups))
        return [pl.BlockSpec((GDN_TB, hw), col(0)),
                pl.BlockSpec((GDN_TB, hw), col(1)),
                pl.BlockSpec((GDN_TB, hw), col(2)),
                pl.BlockSpec((GDN_TB, HEAD_DIM), lambda p, t: (blk(p, t, bwd), p % ngroups)),
                pl.BlockSpec((GDN_GL, GDN_TB), lambda p, t: (p % ngroups, blk(p, t, bwd)))]

    out_spec = lambda bwd: pl.BlockSpec((GDN_TB, hw), lambda p, t: (blk(p, t, bwd), p % ngroups))
    return pl.pallas_call(
        _gdn_scan_kernel,
        grid=(BATCH * ngroups, GDN_STEPS),
        in_specs=specs(False) + specs(True),
        out_specs=[out_spec(False), out_spec(True)],
        out_shape=[jax.ShapeDtypeStruct((ROWS, D), BF16)] * 2,
        scratch_shapes=[pltpu.VMEM((2 * GDN_HG, HEAD_DIM, HEAD_DIM), F32)],
        compiler_params=_params(("parallel", "arbitrary")),
        name="gdn_scan",
    )(qkvg, qkvg, qkvg, gates, gates_t, qkvg, qkvg, qkvg, gates, gates_t)


def _gdn_out_kernel(x_ref, of_ref, ob_ref, zg_ref, ng_ref, gate_ref, w_ref, o_ref):
    zg = zg_ref[...].astype(F32)
    o = of_ref[...].astype(F32) + ob_ref[...].astype(F32)
    y = _head_rms(o, ng_ref[...]) * (zg * jax.nn.sigmoid(zg))
    o_ref[...] = x_ref[...] + gate_ref[...] * jnp.dot(
        y.astype(BF16), w_ref[...], preferred_element_type=F32)


def _gdn_out(x, o_f, o_b, z, norm_g, mods, w_out):
    row = pl.BlockSpec((TM, D), lambda i: (i, 0))
    return pl.pallas_call(
        _gdn_out_kernel,
        grid=(N_TILES,),
        in_specs=[row, row, row,
                  pl.BlockSpec((TM, D), lambda i: (i, GDN_GATE_COL)),
                  pl.BlockSpec((1, HEAD_DIM), lambda i: (0, 0)),
                  _mod_spec(2),
                  pl.BlockSpec((D, D), lambda i: (0, 0))],
        out_specs=row,
        out_shape=jax.ShapeDtypeStruct((ROWS, D), F32),
        compiler_params=_params(("parallel",)),
        name="gdn_out",
    )(x, o_f, o_b, z, norm_g.reshape(1, HEAD_DIM), mods, w_out)


def _gdn_layer(x, mods, g1, w_in, conv_w, a_log, dt_bias, norm_g, w_out):
    ab = w_in[:, 4 * D:].reshape(D, 2, 2, HEADS)
    ab = jnp.transpose(ab, (0, 3, 1, 2)).reshape(D, GDN_GROUPS, GDN_GL)
    wab = jnp.pad(ab, ((0, 0), (0, 0), (0, HEAD_DIM - GDN_GL))).reshape(D, GDN_GW).astype(BF16)

    def pad_head_param(p):
        p = jnp.pad(jnp.transpose(p, (1, 0)), ((0, 0), (0, 2))).reshape(GDN_GROUPS, GDN_GL)
        return jnp.pad(p, ((0, 0), (0, HEAD_DIM - GDN_GL))).reshape(1, GDN_GW).astype(F32)

    qkvg, gates, gates_t = _gdn_proj(x, g1, mods, w_in[:, :4 * D].astype(BF16), wab, conv_w,
                                     pad_head_param(a_log), pad_head_param(dt_bias))
    o_f, o_b = _gdn_scan(qkvg, gates, gates_t)
    return _gdn_out(x, o_f, o_b, qkvg, norm_g, mods, w_out.astype(BF16))


def _sconv_out_kernel(x_ref, bg_ref, cg_ref, hv_ref, cgp_ref, hvp_ref, cgn_ref, hvn_ref,
                      cw_ref, gate_ref, w_ref, o_ref):
    f32 = lambda ref: ref[...].astype(F32)
    prev = (f32(cgp_ref) * f32(hvp_ref))[BF16_HALO - HALO:]
    nxt = (f32(cgn_ref) * f32(hvn_ref))[:HALO]
    xh = jnp.concatenate([prev, f32(cg_ref) * f32(hv_ref), nxt], axis=0)
    y = f32(bg_ref) * _conv3(xh, cw_ref[...], pl.program_id(0) * TM)
    o_ref[...] = x_ref[...] + gate_ref[...] * jnp.dot(
        y.astype(BF16), w_ref[...], preferred_element_type=F32)


def _sconv_layer(x, mods, g1, w_in, conv_w, w_out):
    z = _proj_in(x, g1, mods, w_in.astype(BF16), tn=1024)
    colspec = lambda cidx: pl.BlockSpec((TM, D), lambda i: (i, cidx))
    p1, n1 = _halo_specs(TM, D, lambda i: 1, BF16_HALO)
    p2, n2 = _halo_specs(TM, D, lambda i: 2, BF16_HALO)
    return pl.pallas_call(
        _sconv_out_kernel,
        grid=(N_TILES,),
        in_specs=[colspec(0), colspec(0), colspec(1), colspec(2), p1, p2, n1, n2,
                  pl.BlockSpec((3, D), lambda i: (0, 0)),
                  _mod_spec(2),
                  pl.BlockSpec((D, D), lambda i: (0, 0))],
        out_specs=colspec(0),
        out_shape=jax.ShapeDtypeStruct((ROWS, D), F32),
        compiler_params=_params(("parallel",)),
        name="sconv_out",
    )(x, z, z, z, z, z, z, z, conv_w, mods, w_out.astype(BF16))


def _rope_tables():
    t = jnp.arange(SEQ)
    row = (t // GRID_W).astype(F32)
    colp = (t % GRID_W).astype(F32)
    nf = DIFF_DH // 4
    inv = ROPE_BASE ** (-jnp.arange(nf, dtype=F32) / nf)
    ang = jnp.concatenate([row[:, None] * inv, colp[:, None] * inv], axis=-1)
    ang = jnp.tile(jnp.repeat(ang, 2, axis=-1), (1, HEAD_DIM // DIFF_DH))
    even = (jnp.arange(HEAD_DIM) % 2) == 0
    cos, sin = jnp.cos(ang), jnp.sin(ang)
    return cos, jnp.where(even, -sin, 0.0), jnp.where(even, 0.0, sin)


Q_SCALE = DIFF_DH ** -0.5 * math.log2(math.e)


DIFF_SLAB = 512


def _diff_proj_kernel(x_ref, g_ref, sh_ref, sc_ref, w_ref, wvt_ref, cos_ref, sa_ref, sb_ref,
                      qk_ref, vt_ref, h_ref):
    i, j = pl.program_id(0), pl.program_id(1)
    nt = (((1,), (1,)), ((), ()))

    @pl.when(j == 0)
    def _():
        h_ref[...] = _norm_mod(x_ref[...], g_ref[...], sh_ref[...], sc_ref[...]).astype(BF16)

    def columns(mode):
        scale = jnp.where(j == 0, Q_SCALE, 1.0).astype(F32)
        for c0 in range(0, D, DIFF_SLAB):
            if mode == 'v':
                vt_ref[c0:c0 + DIFF_SLAB, :] = lax.dot_general(
                    wvt_ref[c0:c0 + DIFF_SLAB, :], h_ref[...], nt, preferred_element_type=F32).astype(BF16)
                continue
            z = jnp.dot(h_ref[...], w_ref[:, c0:c0 + DIFF_SLAB], preferred_element_type=F32)
            for h0 in range(0, DIFF_SLAB, HEAD_DIM):
                x = z[:, h0:h0 + HEAD_DIM]
                if mode == 'rope':
                    x = (x * cos_ref[...] + pltpu.roll(x, HEAD_DIM - 1, 1) * sa_ref[...]
                         + pltpu.roll(x, 1, 1) * sb_ref[...])
                qk_ref[:, c0 + h0:c0 + h0 + HEAD_DIM] = (x * scale).astype(BF16)

    is_lat = i < N_LAT_TILES
    pl.when(jnp.logical_and(j < 2, is_lat))(lambda: columns('rope'))
    pl.when(jnp.logical_and(j < 2, jnp.logical_not(is_lat)))(lambda: columns('plain'))
    pl.when(j == 2)(lambda: columns('v'))


def _diff_proj(x, g, mods, w):
    tab = pl.BlockSpec((TM, HEAD_DIM), lambda i, j: (jnp.minimum(i, N_LAT_TILES - 1) % TILES_PER_BATCH, 0))
    return pl.pallas_call(
        _diff_proj_kernel,
        grid=(N_TILES, 3),
        in_specs=[pl.BlockSpec((TM, D), lambda i, j: (i, 0)),
                  pl.BlockSpec((1, D), lambda i, j: (0, 0)),
                  _mod_spec(0), _mod_spec(1),
                  pl.BlockSpec((D, D), lambda i, j: (0, jnp.minimum(j, 1))),
                  pl.BlockSpec((D, D), lambda i, j: (0, 0)),
                  tab, tab, tab],
        out_specs=[pl.BlockSpec((TM, D), lambda i, j: (i, jnp.minimum(j, 1))),
                   pl.BlockSpec((D, TM), lambda i, j: (0, i))],
        out_shape=[jax.ShapeDtypeStruct((ROWS, 2 * D), BF16),
                   jax.ShapeDtypeStruct((D, ROWS), BF16)],
        scratch_shapes=[pltpu.VMEM((TM, D), BF16)],
        compiler_params=_params(("parallel", "arbitrary")),
        name="diff_proj",
    )(x, g.reshape(1, D), mods, mods, w, w[:, 2 * D:].T, *_rope_tables())


def _lambda_full(lam_ref, lambda_init):
    lf = lam_ref[...]
    a = jnp.sum(lf[0:1, :] * lf[1:2, :], axis=-1, keepdims=True)
    b = jnp.sum(lf[2:3, :] * lf[3:4, :], axis=-1, keepdims=True)
    return jnp.exp(a) - jnp.exp(b) + lambda_init


ATT_TQ = 2048
ATT_TK = 1024
ATT_QW = 256


def _split_maps(q):
    lane = lax.broadcasted_iota(jnp.int32, q.shape, 1)
    zero = jnp.zeros_like(q)
    return jnp.where(lane < DIFF_DH, q, zero), jnp.where(lane >= DIFF_DH, q, zero)


ATT_ONES = 16


ATT_CHAINS = [(mi, slice(w * ATT_QW, (w + 1) * ATT_QW))
              for mi in range(2) for w in range(ATT_TQ // ATT_QW)]


def _attn_scores(k, qz_ref, mi, sl):
    nt = (((1,), (1,)), ((), ()))
    return lax.dot_general(k, qz_ref[mi, sl, :], nt, preferred_element_type=F32)


def _attn_consume(s, s_max, vt1, m_ref, acc_ref, mi, sl):
    m_old = m_ref[mi, :, sl]
    m_new = jnp.maximum(m_old, s_max)
    alpha = jnp.exp2(m_old - m_new)
    m_ref[mi, :, sl] = m_new
    p = jnp.exp2((s - m_new).astype(BF16))
    acc_ref[mi, :, sl] = alpha * acc_ref[mi, :, sl] + jnp.dot(vt1, p, preferred_element_type=F32)


def _with_ones(vt):
    return jnp.concatenate([vt, jnp.ones((ATT_ONES, vt.shape[1]), BF16)], axis=0)


def _attn_lat_kernel(lam_ref, q_ref, kc_ref, vct_ref, k_ref, vt_ref, o_ref,
                     qz_ref, m_ref, acc_ref, s0_ref, s1_ref, x0_ref, x1_ref, *, lambda_init, nk):
    j = pl.program_id(2)
    bufs = ((s0_ref, x0_ref), (s1_ref, x1_ref))

    def produce(buf, c, mi, sl):
        s = _attn_scores(k_ref[...], qz_ref, mi, sl)
        buf[0][c] = s
        buf[1][c] = jnp.max(s, axis=0, keepdims=True)

    @pl.when(j == 0)
    def _():
        q1, q2 = _split_maps(q_ref[...])
        qz_ref[0] = q1
        qz_ref[1] = q2
        m_ref[...] = jnp.full_like(m_ref, -jnp.inf)
        acc_ref[...] = jnp.zeros_like(acc_ref)
        for c, (mi, sl) in enumerate(ATT_CHAINS):
            produce(bufs[0], c, mi, sl)

    def step(cur, nxt):
        vt1 = _with_ones(vt_ref[...])
        vct1 = _with_ones(vct_ref[...])
        for c, (mi, sl) in enumerate(ATT_CHAINS):
            if nxt is not None:
                produce(nxt, c, mi, sl)
            _attn_consume(cur[0][c], cur[1][c], vt1, m_ref, acc_ref, mi, sl)
            if nxt is None:
                s = _attn_scores(kc_ref[...], qz_ref, mi, sl)
                _attn_consume(s, jnp.max(s, axis=0, keepdims=True), vct1, m_ref, acc_ref, mi, sl)

    middle = jnp.logical_and(j > 0, j < nk)
    odd = lax.rem(j, 2) == 1
    pl.when(jnp.logical_and(middle, odd))(lambda: step(bufs[0], bufs[1]))
    pl.when(jnp.logical_and(middle, jnp.logical_not(odd)))(lambda: step(bufs[1], bufs[0]))

    @pl.when(j == nk)
    def _():
        step(bufs[(nk - 1) % 2], None)
        lmb = _lambda_full(lam_ref, lambda_init)
        num = lambda mi: acc_ref[mi, :HEAD_DIM, :]
        den = lambda mi: acc_ref[mi, HEAD_DIM:HEAD_DIM + 1, :]
        ot = num(0) / den(0) - lmb * (num(1) / den(1))
        o_ref[...] = ot.T


def _attn_ctx_kernel(lam_ref, q_ref, k_ref, vt_ref, o_ref, *, lambda_init):
    nt = (((1,), (1,)), ((), ()))
    outs = []
    for qz in _split_maps(q_ref[...]):
        s = lax.dot_general(k_ref[...], qz, nt, preferred_element_type=F32)
        p = jnp.exp2(s - jnp.max(s, axis=0, keepdims=True))
        o = jnp.dot(vt_ref[...], p.astype(BF16), preferred_element_type=F32)
        outs.append(o / jnp.sum(p, axis=0, keepdims=True))
    o_ref[...] = (outs[0] - _lambda_full(lam_ref, lambda_init) * outs[1]).T


def _diff_attention(qk, vt, lam, lambda_init):
    nq, nk = SEQ // ATT_TQ, SEQ // ATT_TK
    ctx_blk = LAT_ROWS // CTX
    lam_spec3 = pl.BlockSpec((4, DIFF_DH), lambda p, i, j: (0, 0))
    o_lat = pl.pallas_call(
        functools.partial(_attn_lat_kernel, lambda_init=lambda_init, nk=nk),
        grid=(BATCH * HEADS, nq, nk + 1),
        in_specs=[lam_spec3,
                  pl.BlockSpec((ATT_TQ, HEAD_DIM), lambda p, i, j: ((p // HEADS) * nq + i, p % HEADS)),
                  pl.BlockSpec((CTX, HEAD_DIM), lambda p, i, j: (ctx_blk + p // HEADS, HEADS + p % HEADS)),
                  pl.BlockSpec((HEAD_DIM, CTX), lambda p, i, j: (p % HEADS, ctx_blk + p // HEADS)),
                  pl.BlockSpec((ATT_TK, HEAD_DIM),
                               lambda p, i, j: ((p // HEADS) * nk + jnp.minimum(j, nk - 1), HEADS + p % HEADS)),
                  pl.BlockSpec((HEAD_DIM, ATT_TK),
                               lambda p, i, j: (p % HEADS, (p // HEADS) * nk + jnp.maximum(j - 1, 0)))],
        out_specs=pl.BlockSpec((ATT_TQ, HEAD_DIM), lambda p, i, j: ((p // HEADS) * nq + i, p % HEADS)),
        out_shape=jax.ShapeDtypeStruct((LAT_ROWS, D), F32),
        scratch_shapes=[pltpu.VMEM((2, ATT_TQ, HEAD_DIM), BF16),
                        pltpu.VMEM((2, 1, ATT_TQ), F32),
                        pltpu.VMEM((2, HEAD_DIM + ATT_ONES, ATT_TQ), F32),
                        pltpu.VMEM((len(ATT_CHAINS), ATT_TK, ATT_QW), F32),
                        pltpu.VMEM((len(ATT_CHAINS), ATT_TK, ATT_QW), F32),
                        pltpu.VMEM((len(ATT_CHAINS), 1, ATT_QW), F32),
                        pltpu.VMEM((len(ATT_CHAINS), 1, ATT_QW), F32)],
        compiler_params=_params(("parallel", "parallel", "arbitrary")),
        name="diff_attn_lat",
    )(lam, qk, qk, vt, qk, vt)
    o_ctx = pl.pallas_call(
        functools.partial(_attn_ctx_kernel, lambda_init=lambda_init),
        grid=(BATCH, HEADS),
        in_specs=[pl.BlockSpec((4, DIFF_DH), lambda b, h: (0, 0)),
                  pl.BlockSpec((CTX, HEAD_DIM), lambda b, h: (ctx_blk + b, h)),
                  pl.BlockSpec((CTX, HEAD_DIM), lambda b, h: (ctx_blk + b, HEADS + h)),
                  pl.BlockSpec((HEAD_DIM, CTX), lambda b, h: (h, ctx_blk + b))],
        out_specs=pl.BlockSpec((CTX, HEAD_DIM), lambda b, h: (b, h)),
        out_shape=jax.ShapeDtypeStruct((CTX_ROWS, D), F32),
        compiler_params=_params(("parallel", "parallel")),
        name="diff_attn_ctx",
    )(lam, qk, qk, vt)
    return o_lat, o_ctx


def _diff_out_kernel(x_ref, ol_ref, oc_ref, ng_ref, gate_ref, w_ref, o_ref, *, out_scale):
    is_ctx = pl.program_id(0) >= N_LAT_TILES
    o = jnp.where(is_ctx, oc_ref[...], ol_ref[...])
    y = _head_rms(o, ng_ref[...]) * out_scale
    o_ref[...] = x_ref[...] + gate_ref[...] * jnp.dot(
        y.astype(BF16), w_ref[...], preferred_element_type=F32)


def _diff_layer(x, mods, g1, w_in, lam, norm_g, w_out, lambda_init):
    qk, vt = _diff_proj(x, g1, mods, w_in.astype(BF16))
    o_lat, o_ctx = _diff_attention(qk, vt, lam, lambda_init)
    row = pl.BlockSpec((TM, D), lambda i: (i, 0))
    return pl.pallas_call(
        functools.partial(_diff_out_kernel, out_scale=1.0 - lambda_init),
        grid=(N_TILES,),
        in_specs=[row,
                  pl.BlockSpec((TM, D), lambda i: (jnp.minimum(i, N_LAT_TILES - 1), 0)),
                  pl.BlockSpec((TM, D), lambda i: (0, 0)),
                  pl.BlockSpec((1, HEAD_DIM), lambda i: (0, 0)),
                  _mod_spec(2),
                  pl.BlockSpec((D, D), lambda i: (0, 0))],
        out_specs=row,
        out_shape=jax.ShapeDtypeStruct((ROWS, D), F32),
        compiler_params=_params(("parallel",)),
        name="diff_out",
    )(x, o_lat, o_ctx, norm_g.reshape(1, HEAD_DIM), mods, w_out.astype(BF16))


def kernel(x, c, ctx, c_ctx, norm1_g, norm2_g, ada_w, ada_b, mlp_w1, mlp_w2, gdn_w_in, gdn_conv, gdn_a_log, gdn_dt_bias, gdn_norm_g, gdn_w_out, sconv_w_in, sconv_conv, sconv_w_out, diff_w_in, diff_lambda, diff_norm_g, diff_w_out, final_g):
    mods = _ada(c, c_ctx, ada_w, ada_b)
    h = jnp.concatenate([x.reshape(LAT_ROWS, D), ctx.reshape(CTX_ROWS, D)], axis=0)
    for i in range(DEPTH):
        kind, j = i % 3, i // 3
        if kind == 0:
            h = _gdn_layer(h, mods[i], norm1_g[i], gdn_w_in[j], gdn_conv[j], gdn_a_log[j],
                           gdn_dt_bias[j], gdn_norm_g[j], gdn_w_out[j])
        elif kind == 1:
            h = _sconv_layer(h, mods[i], norm1_g[i], sconv_w_in[j], sconv_conv[j], sconv_w_out[j])
        else:
            lambda_init = 0.8 - 0.6 * math.exp(-0.3 * i)
            h = _diff_layer(h, mods[i], norm1_g[i], diff_w_in[j], diff_lambda[j],
                            diff_norm_g[j], diff_w_out[j], lambda_init)
        h = _mlp(h, norm2_g[i], mods[i], mlp_w1[i].astype(BF16), mlp_w2[i].astype(BF16),
                 final_g, last=i == DEPTH - 1)
    return h.reshape(BATCH, SEQ, D)
```
